```python
import numpy as np
import jax
import jax.numpy as jnp
from jax import lax

D_MODEL = 1024
BATCH = 2
SEQ = 8192
DEPTH = 4

PLE_DIM = 256
D_FF = 2816
ROPE_THETA = 500000.0
Q_BLOCK = 128
HEAD_DIM = 64
ROT_DIM = HEAD_DIM // 4
NEG_INF = -1e30
RMS_EPS = 1e-6
A_HEADS = 8
A_KV_RANK = 128
IDX_HEADS = 8
IDX_DIM = 32
IDX_ROT_DIM = IDX_DIM // 4
A_TOPK_MAX = 256
B_HEADS = 8
B_GROUPS = 2
B_HPG = B_HEADS // B_GROUPS
CMP_BLOCK = 32
CMP_STRIDE = 16
CMP_HIDDEN = 256
SEL_BLOCK = 64
SEL_TOPN = 16
WINDOW = 512
FORCED_SCORE = 1e4
N_BRANCH_GATES = 3
A_Q_W = A_HEADS * HEAD_DIM
B_Q_W = B_HEADS * HEAD_DIM
B_KV_W = B_GROUPS * HEAD_DIM
IN_SIZES = (A_Q_W, A_KV_RANK, IDX_HEADS * IDX_DIM, IDX_DIM, IDX_HEADS, B_Q_W, 6 * B_KV_W, B_HEADS * N_BRANCH_GATES, 2 * D_MODEL)
IN_SPLITS = tuple(int(v) for v in np.cumsum(IN_SIZES)[:-1])
D_IN = int(sum(IN_SIZES))

kernel_name = 'hybrid_dsa_nsa_macaron_trunk'


def rms_norm(x, g):
    xf = x.astype(jnp.float32)
    y = xf * lax.rsqrt(jnp.mean(xf * xf, axis=-1, keepdims=True) + RMS_EPS)
    return (y * g.astype(jnp.float32)).astype(x.dtype)


def swiglu(h, w1, w3, w2):
    return (jax.nn.silu(h @ w1) * (h @ w3)) @ w2


def rope_tables(seq, rot_dim):
    pos = jnp.arange(seq, dtype=jnp.float32)
    inv = ROPE_THETA ** (-jnp.arange(0, rot_dim, 2, dtype=jnp.float32) / rot_dim)
    ang = pos[:, None] * inv[None, :]
    return jnp.cos(ang), jnp.sin(ang)


def partial_rope(x, cos, sin):
    half = cos.shape[-1]
    x1, x2, xp = x[..., :half], x[..., half:2 * half], x[..., 2 * half:]
    shape = (1, cos.shape[0]) + (1,) * (x.ndim - 3) + (half,)
    c = cos.reshape(shape).astype(x.dtype)
    s = sin.reshape(shape).astype(x.dtype)
    return jnp.concatenate([x1 * c - x2 * s, x2 * c + x1 * s, xp], axis=-1)


def masked_softmax(scores, mask):
    s = jnp.where(mask, scores.astype(jnp.float32), NEG_INF)
    return jax.nn.softmax(s, axis=-1) * mask


def dsa_attention(q, k, v, q_idx, k_idx, w_idx, top_k):
    B, S = q.shape[0], q.shape[1]
    scale = HEAD_DIM ** -0.5
    bi = jnp.arange(B)[:, None, None]
    key_pos = jnp.arange(S)

    def block(qb):
        q0 = qb * Q_BLOCK
        t = q0 + jnp.arange(Q_BLOCK)
        qi = lax.dynamic_slice_in_dim(q_idx, q0, Q_BLOCK, axis=1)
        wi = lax.dynamic_slice_in_dim(w_idx, q0, Q_BLOCK, axis=1)
        qh = lax.dynamic_slice_in_dim(q, q0, Q_BLOCK, axis=1)
        rel = jax.nn.relu(jnp.einsum('bqhd,bsd->bqhs', qi, k_idx).astype(jnp.float32))
        score = jnp.einsum('bqh,bqhs->bqs', wi.astype(jnp.float32), rel)
        causal = key_pos[None, :] <= t[:, None]
        score = jnp.where(causal[None], score, -jnp.inf)
        _, idx = lax.top_k(score, top_k)
        ks = k[bi, idx]
        vs = v[bi, idx]
        valid = idx <= t[None, :, None]
        s = jnp.einsum('bqhd,bqkd->bqhk', qh, ks) * scale
        pr = masked_softmax(s, valid[:, :, None, :])
        return jnp.einsum('bqhk,bqkd->bqhd', pr.astype(v.dtype), vs)

    out = lax.map(block, jnp.arange(S // Q_BLOCK))
    return out.transpose(1, 0, 2, 3, 4).reshape(B, S, -1)


def nsa_attention(q, k_cmp, v_cmp, k_slc, v_slc, k_win, v_win, gates, cmp_pos, wk1, wk2, wv1, wv2):
    B, S = q.shape[0], q.shape[1]
    scale = HEAD_DIM ** -0.5
    n_cmp = (S - CMP_BLOCK) // CMP_STRIDE + 1
    n_blk = S // SEL_BLOCK
    n_top = min(SEL_TOPN, n_blk)
    tok = jnp.arange(n_cmp)[:, None] * CMP_STRIDE + jnp.arange(CMP_BLOCK)[None, :]

    def compress(t_in, w1, w2):
        blk = t_in[:, tok] + cmp_pos[None, None, :, None, :]
        blk = blk.transpose(0, 1, 3, 2, 4).reshape(B, n_cmp, B_GROUPS, CMP_BLOCK * HEAD_DIM)
        return jax.nn.silu(blk @ w1) @ w2

    kc = compress(k_cmp, wk1, wk2)
    vc = compress(v_cmp, wv1, wv2)
    c_start = jnp.arange(n_cmp) * CMP_STRIDE
    cmp_end = c_start + CMP_BLOCK - 1
    s_start = jnp.arange(n_blk) * SEL_BLOCK
    overlap = ((c_start[:, None] < s_start[None, :] + SEL_BLOCK) &
               (s_start[None, :] < c_start[:, None] + CMP_BLOCK)).astype(jnp.float32)
    kb = k_slc.reshape(B, n_blk, SEL_BLOCK, B_GROUPS, HEAD_DIM)
    vb = v_slc.reshape(B, n_blk, SEL_BLOCK, B_GROUPS, HEAD_DIM)
    pad = jnp.zeros((B, WINDOW, B_GROUPS, HEAD_DIM), k_win.dtype)
    kw = jnp.concatenate([pad, k_win], axis=1)
    vw = jnp.concatenate([pad, v_win], axis=1)
    bi = jnp.arange(B)[:, None, None, None]
    gi = jnp.arange(B_GROUPS)[None, None, :, None]
    blk_ids = jnp.arange(n_blk)

    def block(qb):
        q0 = qb * Q_BLOCK
        t = q0 + jnp.arange(Q_BLOCK)
        qh = lax.dynamic_slice_in_dim(q, q0, Q_BLOCK, axis=1)
        g = lax.dynamic_slice_in_dim(gates, q0, Q_BLOCK, axis=1)
        s = jnp.einsum('bqghd,bcgd->bqghc', qh, kc) * scale
        cmask = cmp_end[None, :] <= t[:, None]
        p_c = masked_softmax(s, cmask[None, :, None, None, :])
        o_c = jnp.einsum('bqghc,bcgd->bqghd', p_c.astype(vc.dtype), vc)
        imp = jnp.einsum('bqghc,cn->bqgn', p_c, overlap)
        forced = (blk_ids[None, :] == (t // SEL_BLOCK)[:, None]) | (blk_ids[None, :] == 0)
        admissible = blk_ids[None, :] * SEL_BLOCK <= t[:, None]
        imp = jnp.where(forced[None, :, None, :], FORCED_SCORE,
                        jnp.where(admissible[None, :, None, :], imp, -1.0))
        _, sel = lax.top_k(imp, n_top)
        ks = kb[bi, sel, :, gi].reshape(B, Q_BLOCK, B_GROUPS, n_top * SEL_BLOCK, HEAD_DIM)
        vs = vb[bi, sel, :, gi].reshape(B, Q_BLOCK, B_GROUPS, n_top * SEL_BLOCK, HEAD_DIM)
        spos = (sel[..., None] * SEL_BLOCK + jnp.arange(SEL_BLOCK)).reshape(B, Q_BLOCK, B_GROUPS, n_top * SEL_BLOCK)
        smask = spos <= t[None, :, None, None]
        s = jnp.einsum('bqghd,bqgkd->bqghk', qh, ks) * scale
        p_s = masked_softmax(s, smask[:, :, :, None, :])
        o_s = jnp.einsum('bqghk,bqgkd->bqghd', p_s.astype(vs.dtype), vs)
        kwb = lax.dynamic_slice_in_dim(kw, q0, Q_BLOCK + WINDOW, axis=1)
        vwb = lax.dynamic_slice_in_dim(vw, q0, Q_BLOCK + WINDOW, axis=1)
        wpos = q0 - WINDOW + jnp.arange(Q_BLOCK + WINDOW)
        wmask = (wpos[None, :] <= t[:, None]) & (wpos[None, :] > t[:, None] - WINDOW) & (wpos[None, :] >= 0)
        s = jnp.einsum('bqghd,bkgd->bqghk', qh, kwb) * scale
        p_w = masked_softmax(s, wmask[None, :, None, None, :])
        o_w = jnp.einsum('bqghk,bkgd->bqghd', p_w.astype(vwb.dtype), vwb)
        return g[..., 0:1] * o_c + g[..., 1:2] * o_s + g[..., 2:3] * o_w

    out = lax.map(block, jnp.arange(S // Q_BLOCK))
    return out.transpose(1, 0, 2, 3, 4, 5).reshape(B, S, -1)


def token_mixer(h, rope, rope_idx, w_in, a_kv_norm, a_w_ukv, cmp_pos, cmp_k_w1, cmp_k_w2,
                cmp_v_w1, cmp_v_w2, w_proj_a, w_proj_b, w_out):
    B, S = h.shape[0], h.shape[1]
    cos, sin = rope
    cos_i, sin_i = rope_idx
    qa, ckv, qi, ki, wi, qb, kvb, gb, gm = jnp.split(h @ w_in, IN_SPLITS, axis=-1)
    qa = partial_rope(qa.reshape(B, S, A_HEADS, HEAD_DIM), cos, sin)
    ka, va = jnp.split(rms_norm(ckv, a_kv_norm) @ a_w_ukv, 2, axis=-1)
    ka = partial_rope(ka, cos, sin)
    qi = partial_rope(qi.reshape(B, S, IDX_HEADS, IDX_DIM), cos_i, sin_i)
    ki = partial_rope(ki, cos_i, sin_i)
    wi = wi * IDX_HEADS ** -0.5
    top_k = min(A_TOPK_MAX, S // 4)
    o_a = dsa_attention(qa, ka, va, qi, ki, wi, top_k)
    qb = partial_rope(qb.reshape(B, S, B_GROUPS, B_HPG, HEAD_DIM), cos, sin)
    kc, vc, ks, vs, kw, vw = [t.reshape(B, S, B_GROUPS, HEAD_DIM) for t in jnp.split(kvb, 6, axis=-1)]
    kc = partial_rope(kc, cos, sin)
    ks = partial_rope(ks, cos, sin)
    kw = partial_rope(kw, cos, sin)
    gates = jax.nn.sigmoid(gb).reshape(B, S, B_GROUPS, B_HPG, N_BRANCH_GATES)
    o_b = nsa_attention(qb, kc, vc, ks, vs, kw, vw, gates, cmp_pos,
                        cmp_k_w1, cmp_k_w2, cmp_v_w1, cmp_v_w2)
    g_a, g_b = jnp.split(jax.nn.sigmoid(gm), 2, axis=-1)
    merged = g_a * (o_a @ w_proj_a) + g_b * (o_b @ w_proj_b)
    return merged @ w_out


def setup_inputs(seed: int = 0) -> dict:
    key = jax.random.key(seed)
    keys = iter(jax.random.split(key, 32))
    L = DEPTH

    def dense(shape, fan_in):
        return jax.random.normal(next(keys), shape, jnp.float32) * fan_in ** -0.5

    def gain(shape):
        return 1.0 + 0.02 * jax.random.normal(next(keys), shape, jnp.float32)

    return {
        'x': jax.random.normal(next(keys), (BATCH, SEQ, D_MODEL), jnp.float32),
        'p': jax.random.normal(next(keys), (DEPTH, BATCH, SEQ, PLE_DIM), jnp.float32),
        'ffa_norm': gain((L, D_MODEL)),
        'ffa_w1': dense((L, D_MODEL, D_FF), D_MODEL),
        'ffa_w3': dense((L, D_MODEL, D_FF), D_MODEL),
        'ffa_w2': dense((L, D_FF, D_MODEL), D_FF),
        'mix_norm': gain((L, D_MODEL)),
        'w_in': dense((L, D_MODEL, D_IN), D_MODEL),
        'a_kv_norm': gain((L, A_KV_RANK)),
        'a_w_ukv': dense((L, A_KV_RANK, 2 * HEAD_DIM), A_KV_RANK),
        'cmp_pos': 0.1 * jax.random.normal(next(keys), (L, CMP_BLOCK, HEAD_DIM), jnp.float32),
        'cmp_k_w1': dense((L, CMP_BLOCK * HEAD_DIM, CMP_HIDDEN), CMP_BLOCK * HEAD_DIM),
        'cmp_k_w2': dense((L, CMP_HIDDEN, HEAD_DIM), CMP_HIDDEN),
        'cmp_v_w1': dense((L, CMP_BLOCK * HEAD_DIM, CMP_HIDDEN), CMP_BLOCK * HEAD_DIM),
        'cmp_v_w2': dense((L, CMP_HIDDEN, HEAD_DIM), CMP_HIDDEN),
        'w_proj_a': dense((L, A_Q_W, D_MODEL), A_Q_W),
        'w_proj_b': dense((L, B_Q_W, D_MODEL), B_Q_W),
        'w_out': dense((L, D_MODEL, D_MODEL), D_MODEL),
        'ffb_norm': gain((L, D_MODEL)),
        'ffb_w1': dense((L, D_MODEL, D_FF), D_MODEL),
        'ffb_w3': dense((L, D_MODEL, D_FF), D_MODEL),
        'ffb_w2': dense((L, D_FF, D_MODEL), D_FF),
        'ple_norm': gain((L, D_MODEL)),
        'ple_w_gate': dense((L, D_MODEL, D_MODEL), D_MODEL),
        'ple_w_proj': dense((L, PLE_DIM, D_MODEL), PLE_DIM),
        'final_norm': gain((D_MODEL,)),
    }


def reference(x, p, ffa_norm, ffa_w1, ffa_w3, ffa_w2, mix_norm, w_in, a_kv_norm, a_w_ukv,
              cmp_pos, cmp_k_w1, cmp_k_w2, cmp_v_w1, cmp_v_w2, w_proj_a, w_proj_b, w_out,
              ffb_norm, ffb_w1, ffb_w3, ffb_w2, ple_norm, ple_w_gate, ple_w_proj, final_norm):
    S = x.shape[1]
    rope = rope_tables(S, ROT_DIM)
    rope_idx = rope_tables(S, IDX_ROT_DIM)
    for i in range(DEPTH):
        x = x + 0.5 * swiglu(rms_norm(x, ffa_norm[i]), ffa_w1[i], ffa_w3[i], ffa_w2[i])
        x = x + token_mixer(rms_norm(x, mix_norm[i]), rope, rope_idx, w_in[i], a_kv_norm[i], a_w_ukv[i],
                            cmp_pos[i], cmp_k_w1[i], cmp_k_w2[i], cmp_v_w1[i], cmp_v_w2[i],
                            w_proj_a[i], w_proj_b[i], w_out[i])
        x = x + 0.5 * swiglu(rms_norm(x, ffb_norm[i]), ffb_w1[i], ffb_w3[i], ffb_w2[i])
        gate = jax.nn.sigmoid(rms_norm(x, ple_norm[i]) @ ple_w_gate[i])
        x = x + gate * (p[i] @ ple_w_proj[i])
    return rms_norm(x, final_norm)
```

```python
import functools

import numpy as np
import jax
import jax.numpy as jnp
from jax import lax
from jax.experimental import pallas as pl
from jax.experimental.pallas import tpu as pltpu

F32 = jnp.float32
BF16 = jnp.bfloat16
I32 = jnp.int32

D_MODEL = 1024
DEPTH = 4
PLE_DIM = 256
D_FF = 2816
ROPE_THETA = 500000.0
HEAD_DIM = 64
ROT_DIM = HEAD_DIM // 4
NEG_INF = -1e30
RMS_EPS = 1e-6
A_HEADS = 8
A_KV_RANK = 128
IDX_HEADS = 8
IDX_DIM = 32
IDX_ROT_DIM = IDX_DIM // 4
A_TOPK_MAX = 256
B_HEADS = 8
B_GROUPS = 2
B_HPG = B_HEADS // B_GROUPS
CMP_BLOCK = 32
CMP_STRIDE = 16
CMP_HIDDEN = 256
SEL_BLOCK = 64
SEL_TOPN = 16
WINDOW = 512
FORCED_SCORE = 1e4
N_BRANCH_GATES = 3

LANES = 128
QB = 128
KCH = 512
NSEL = 128
INT_MIN = -2147483648

OFF_QA, OFF_QB, OFF_KVB, OFF_GM = 0, 512, 1024, 2048
OFF_QI, OFF_CKV, OFF_KI, OFF_WI, OFF_GB = 4096, 4352, 4480, 4608, 4736
Z_COLS = 4864
VMEM_LIMIT = 56 * 1024 * 1024


def _nt(a, b):
    return lax.dot_general(a, b, (((1,), (1,)), ((), ())), preferred_element_type=F32)


def _nn(a, b):
    return jnp.dot(a, b, preferred_element_type=F32)


def _cparams(sem):
    return pltpu.CompilerParams(dimension_semantics=sem, vmem_limit_bytes=VMEM_LIMIT)


def _rms(x, g):
    ms = jnp.mean(x * x, axis=-1, keepdims=True)
    return x * lax.rsqrt(ms + RMS_EPS) * g


def _sigmoid(x):
    return 1.0 / (1.0 + jnp.exp(-x))


def _ffn_body(x_ref, g_ref, w1_ref, w3_ref, w2_ref, o_ref, h_scr, acc_scr):
    j = pl.program_id(1)

    @pl.when(j == 0)
    def _():
        h_scr[...] = _rms(x_ref[...], g_ref[...]).astype(BF16)
        acc_scr[...] = jnp.zeros_like(acc_scr)

    h = h_scr[...]
    a = _nn(h, w1_ref[...])
    b = _nn(h, w3_ref[...])
    u = (a * _sigmoid(a)) * b
    acc_scr[...] += _nn(u.astype(BF16), w2_ref[...])

    @pl.when(j == pl.num_programs(1) - 1)
    def _():
        o_ref[...] = x_ref[...] + 0.5 * acc_scr[...]


def _ffn(x, g, w1, w3, w2, tm=512, tf=1408):
    m, d = x.shape
    f = w1.shape[1]
    return pl.pallas_call(
        _ffn_body,
        grid=(m // tm, f // tf),
        in_specs=[
            pl.BlockSpec((tm, d), lambda i, j: (i, 0)),
            pl.BlockSpec((1, d), lambda i, j: (0, 0)),
            pl.BlockSpec((d, tf), lambda i, j: (0, j)),
            pl.BlockSpec((d, tf), lambda i, j: (0, j)),
            pl.BlockSpec((tf, d), lambda i, j: (j, 0)),
        ],
        out_specs=pl.BlockSpec((tm, d), lambda i, j: (i, 0)),
        out_shape=jax.ShapeDtypeStruct((m, d), F32),
        scratch_shapes=[pltpu.VMEM((tm, d), BF16), pltpu.VMEM((tm, d), F32)],
        compiler_params=_cparams(("parallel", "arbitrary")),
        name="ffn",
    )(x, g, w1, w3, w2)


def _inproj_body(x_ref, g_ref, w_ref, o_ref, h_scr):
    @pl.when(pl.program_id(1) == 0)
    def _():
        h_scr[...] = _rms(x_ref[...], g_ref[...]).astype(BF16)

    o_ref[...] = _nn(h_scr[...], w_ref[...])


def _inproj(x, g, w, tm=512, tn=2432):
    m, d = x.shape
    n = w.shape[1]
    return pl.pallas_call(
        _inproj_body,
        grid=(m // tm, n // tn),
        in_specs=[
            pl.BlockSpec((tm, d), lambda i, j: (i, 0)),
            pl.BlockSpec((1, d), lambda i, j: (0, 0)),
            pl.BlockSpec((d, tn), lambda i, j: (0, j)),
        ],
        out_specs=pl.BlockSpec((tm, tn), lambda i, j: (i, j)),
        out_shape=jax.ShapeDtypeStruct((m, n), F32),
        scratch_shapes=[pltpu.VMEM((tm, d), BF16)],
        compiler_params=_cparams(("parallel", "arbitrary")),
        name="inproj",
    )(x, g, w)


def _rope_tables(seq):
    pos = jnp.arange(seq, dtype=F32)
    lane = np.arange(LANES)
    out = []
    for rot_dim, period, first_only in ((ROT_DIM, HEAD_DIM, False),
                                        (ROT_DIM, HEAD_DIM, True),
                                        (IDX_ROT_DIM, IDX_DIM, False)):
        half = rot_dim // 2
        inv = ROPE_THETA ** (-jnp.arange(0, rot_dim, 2, dtype=F32) / rot_dim)
        ang = pos[:, None] * inv[None, :]
        cos, sin = jnp.cos(ang), jnp.sin(ang)
        lp = lane % period
        in1 = lp < half
        in2 = (lp >= half) & (lp < 2 * half)
        if first_only:
            in1 = in1 & (lane < period)
            in2 = in2 & (lane < period)
        fidx = np.where(in1, lp, np.where(in2, lp - half, 0))
        cos_l, sin_l = cos[:, fidx], sin[:, fidx]
        out.append(jnp.where((in1 | in2)[None, :], cos_l, 1.0))
        out.append(jnp.where(in1[None, :], -sin_l, 0.0))
        out.append(jnp.where(in2[None, :], sin_l, 0.0))
    return jnp.stack(out).astype(F32)


def _rope(x, tab_ref, k, half):
    a, b, c = tab_ref[3 * k], tab_ref[3 * k + 1], tab_ref[3 * k + 2]
    return (x * a + pltpu.roll(x, LANES - half, axis=1) * b
            + pltpu.roll(x, half, axis=1) * c)


def _prep_body(qa_ref, qb_ref, kvb_ref, qi_ref, ckv_ref, ki_ref, wi_ref, gb_ref,
               tab_ref, kvn_ref, ukv_ref,
               aq_ref, ak_ref, av_ref, iq_ref, ik_ref, iw_ref,
               bq_ref, ck_ref, cv_ref, sk_ref, sv_ref, wk_ref, wv_ref, gt_ref):
    tr = qa_ref.shape[0]
    nq = tr // QB
    lane = lax.broadcasted_iota(I32, (tr, LANES), 1)
    lo = lane < HEAD_DIM
    scale = HEAD_DIM ** -0.5
    half = ROT_DIM // 2

    for cg in range(4):
        sl = slice(cg * LANES, (cg + 1) * LANES)
        ya = _rope(qa_ref[:, sl], tab_ref, 0, half) * scale
        ya_sw = pltpu.roll(ya, HEAD_DIM, axis=1)
        even = jnp.where(lo, ya, 0.0).astype(BF16)
        odd = jnp.where(lo, ya_sw, 0.0).astype(BF16)
        yb = _rope(qb_ref[:, sl], tab_ref, 0, half) * scale
        yb_sw = pltpu.roll(yb, HEAD_DIM, axis=1)
        if cg < 2:
            b_even = jnp.where(lo, yb, 0.0).astype(BF16)
            b_odd = jnp.where(lo, yb_sw, 0.0).astype(BF16)
        else:
            b_even = jnp.where(lo, 0.0, yb_sw).astype(BF16)
            b_odd = jnp.where(lo, 0.0, yb).astype(BF16)
        for q in range(nq):
            rows = slice(q * QB, (q + 1) * QB)
            aq_ref[q, 2 * cg] = even[rows]
            aq_ref[q, 2 * cg + 1] = odd[rows]
            bq_ref[q, 2 * cg] = b_even[rows]
            bq_ref[q, 2 * cg + 1] = b_odd[rows]

    first = lane < IDX_DIM
    for cg in range(2):
        y = _rope(qi_ref[:, cg * LANES:(cg + 1) * LANES], tab_ref, 2, IDX_ROT_DIM // 2)
        for k in range(4):
            yk = y if k == 0 else pltpu.roll(y, LANES - IDX_DIM * k, axis=1)
            piece = jnp.where(first, yk, 0.0).astype(BF16)
            for q in range(nq):
                iq_ref[q, 4 * cg + k] = piece[q * QB:(q + 1) * QB]

    ik_ref[...] = _rope(ki_ref[...], tab_ref, 2, IDX_ROT_DIM // 2).astype(BF16)
    iw_ref[...] = wi_ref[...] * (IDX_HEADS ** -0.5)

    c = _rms(ckv_ref[...], kvn_ref[...]).astype(BF16)
    kv = _rope(_nn(c, ukv_ref[...]), tab_ref, 1, half)
    ak_ref[...] = jnp.where(lo, kv, 0.0).astype(BF16)
    av_ref[...] = jnp.where(lo, pltpu.roll(kv, HEAD_DIM, axis=1), 1.0).astype(BF16)

    ck_ref[...] = _rope(kvb_ref[:, 0:128], tab_ref, 0, half)
    cv_ref[...] = kvb_ref[:, 128:256]
    sk_ref[...] = _rope(kvb_ref[:, 256:384], tab_ref, 0, half).astype(BF16)
    vs = kvb_ref[:, 384:512]
    wk_ref[...] = _rope(kvb_ref[:, 512:640], tab_ref, 0, half).astype(BF16)
    vw = kvb_ref[:, 640:768]
    sv_ref[0] = jnp.where(lo, vs, 1.0).astype(BF16)
    sv_ref[1] = jnp.where(lo, pltpu.roll(vs, HEAD_DIM, axis=1), 1.0).astype(BF16)
    wv_ref[0] = jnp.where(lo, vw, 1.0).astype(BF16)
    wv_ref[1] = jnp.where(lo, pltpu.roll(vw, HEAD_DIM, axis=1), 1.0).astype(BF16)
    gt_ref[...] = _sigmoid(gb_ref[...])


def _prep(z, tabs, kvn, ukv, seq, tr=512):
    m = z.shape[0]
    nq = tr // QB
    nqb_total = m // QB
    spt = seq // tr

    def zspec(width, off):
        return pl.BlockSpec((tr, width), lambda i, o=off // width: (i, o))

    row = lambda dt: jax.ShapeDtypeStruct((m, LANES), dt)
    rspec = pl.BlockSpec((tr, LANES), lambda i: (i, 0))
    stack = jax.ShapeDtypeStruct((nqb_total, 8, QB, LANES), BF16)
    sspec = pl.BlockSpec((nq, 8, QB, LANES), lambda i: (i, 0, 0, 0))
    pair = jax.ShapeDtypeStruct((2, m, LANES), BF16)
    pspec = pl.BlockSpec((2, tr, LANES), lambda i: (0, i, 0))
    return pl.pallas_call(
        _prep_body,
        grid=(m // tr,),
        in_specs=[
            zspec(512, OFF_QA), zspec(512, OFF_QB), zspec(1024, OFF_KVB),
            zspec(256, OFF_QI), zspec(128, OFF_CKV), zspec(128, OFF_KI),
            zspec(128, OFF_WI), zspec(128, OFF_GB),
            pl.BlockSpec((9, tr, LANES), lambda i: (0, i % spt, 0)),
            pl.BlockSpec((1, LANES), lambda i: (0, 0)),
            pl.BlockSpec((LANES, LANES), lambda i: (0, 0)),
        ],
        out_specs=[sspec, rspec, rspec, sspec, rspec, rspec,
                   sspec, rspec, rspec, rspec, pspec, rspec, pspec, rspec],
        out_shape=[stack, row(BF16), row(BF16), stack, row(BF16), row(F32),
                   stack, row(F32), row(F32), row(BF16), pair, row(BF16), pair, row(F32)],
        compiler_params=_cparams(("parallel",)),
        name="prep",
    )(z, z, z, z, z, z, z, z, tabs, kvn, ukv)


def _compress_body(xk_ref, xv_ref, pos_ref, wk1_ref, wk2_ref, wv1_ref, wv2_ref, kc_ref, vc_ref):
    nh = xk_ref.shape[0]

    def one(x_ref, w1_ref, w2_ref, o_ref):
        x = x_ref[...]
        top = (x + pos_ref[0:1, :]).astype(BF16)
        bot = (x + pos_ref[1:2, :]).astype(BF16)
        out = jnp.zeros((nh, LANES), F32)
        for g in range(B_GROUPS):
            a = _nn(top, w1_ref[g, 0])
            b = _nn(bot, w1_ref[g, 1])
            hid = a + pltpu.roll(b, nh - 1, axis=0)
            act = (hid * _sigmoid(hid)).astype(BF16)
            out = out + _nn(act, w2_ref[g])
        o_ref[...] = out.astype(BF16)

    one(xk_ref, wk1_ref, wk2_ref, kc_ref)
    one(xv_ref, wv1_ref, wv2_ref, vc_ref)


def _compress(ck, cv, pos2, wk1, wk2, wv1, wv2, batch, seq):
    nh = seq // CMP_STRIDE
    width = CMP_STRIDE * LANES
    xk = ck.reshape(batch * nh, width)
    xv = cv.reshape(batch * nh, width)
    xspec = pl.BlockSpec((nh, width), lambda b: (b, 0))
    w1spec = pl.BlockSpec((B_GROUPS, 2, width, CMP_HIDDEN), lambda b: (0, 0, 0, 0))
    w2spec = pl.BlockSpec((B_GROUPS, CMP_HIDDEN, LANES), lambda b: (0, 0, 0))
    ospec = pl.BlockSpec((nh, LANES), lambda b: (b, 0))
    oshape = jax.ShapeDtypeStruct((batch * nh, LANES), BF16)
    return pl.pallas_call(
        _compress_body,
        grid=(batch,),
        in_specs=[xspec, xspec, pl.BlockSpec((2, width), lambda b: (0, 0)),
                  w1spec, w2spec, w1spec, w2spec],
        out_specs=[ospec, ospec],
        out_shape=[oshape, oshape],
        compiler_params=_cparams(("parallel",)),
        name="compress",
    )(xk, xv, pos2, wk1, wk2, wv1, wv2)


def _sort_key(x):
    x = jnp.where(x == 0.0, 0.0, x)
    bits = lax.bitcast_convert_type(x, I32)
    return bits ^ ((bits >> 31) & 0x7FFFFFFF)


def _row_total(cnt):
    return jnp.broadcast_to(jnp.sum(cnt, axis=1, keepdims=True), cnt.shape)


def _kth_largest_key(count_ge, k):
    def step(it, tau):
        cand = tau + lax.shift_left(jnp.int32(1), 31 - it)
        return jnp.where(count_ge(cand) >= k, cand, tau)

    tau0 = jnp.full((QB, LANES), INT_MIN, I32)
    return lax.fori_loop(0, 32, step, tau0)


def _dsa_body(aq_ref, iq_ref, iw_ref, ak_ref, av_ref, ik_ref, tri_ref, o_ref,
              key_scr, wb_scr, m_scr, acc_scr, *, top_k):
    i = pl.program_id(1)
    q0 = i * QB
    nch = (q0 + QB - 1) // KCH + 1
    gpc = KCH // LANES
    rows = A_HEADS * QB
    row_t = q0 + lax.broadcasted_iota(I32, (QB, LANES), 0)
    lane = lax.broadcasted_iota(I32, (QB, LANES), 1)

    iw = iw_ref[...]
    for h in range(IDX_HEADS):
        wb_scr[h] = jnp.broadcast_to(iw[:, h:h + 1], (QB, LANES))
    iq = iq_ref[0].reshape(rows, LANES)

    def score_chunk(c, carry):
        base = pl.multiple_of(c * KCH, KCH)
        r = _nt(iq, ik_ref[pl.ds(base, KCH), :])
        for j in range(gpc):
            sc = jnp.zeros((QB, LANES), F32)
            for h in range(IDX_HEADS):
                blk = r[h * QB:(h + 1) * QB, j * LANES:(j + 1) * LANES]
                sc = sc + jnp.maximum(blk, 0.0) * wb_scr[h]
            s_idx = base + j * LANES + lane
            sc = jnp.where(s_idx <= row_t, sc, -jnp.inf)
            key_scr[c * gpc + j] = _sort_key(sc)
        return carry

    lax.fori_loop(0, nch, score_chunk, 0)

    def count_ge(cand):
        def body(g, cnt):
            return cnt + jnp.where(key_scr[g] >= cand, 1.0, 0.0)
        return _row_total(lax.fori_loop(0, nch * gpc, body, jnp.zeros((QB, LANES), F32)))

    tau = _kth_largest_key(count_ge, float(top_k))

    def gt_body(g, cnt):
        return cnt + jnp.where(key_scr[g] > tau, 1.0, 0.0)
    n_gt = _row_total(lax.fori_loop(0, nch * gpc, gt_body, jnp.zeros((QB, LANES), F32)))
    need = float(top_k) - n_gt

    aq = aq_ref[0].reshape(rows, LANES)
    m_scr[...] = jnp.full(m_scr.shape, NEG_INF, F32)
    acc_scr[...] = jnp.zeros_like(acc_scr)
    tri = tri_ref[...]

    def attn_chunk(c, tie_seen):
        base = pl.multiple_of(c * KCH, KCH)
        s3 = _nt(aq, ak_ref[pl.ds(base, KCH), :]).reshape(A_HEADS, QB, KCH)
        pieces = []
        for j in range(gpc):
            key = key_scr[c * gpc + j]
            eq = key == tau
            eqf = jnp.where(eq, 1.0, 0.0)
            rank = tie_seen + _nn(eqf.astype(BF16), tri)
            tie_seen = tie_seen + _row_total(eqf)
            s_idx = base + j * LANES + lane
            sel = ((key > tau) | (eq & (rank <= need))) & (s_idx <= row_t)
            pieces.append(jnp.where(sel[None], s3[:, :, j * LANES:(j + 1) * LANES], NEG_INF))
        sm = jnp.concatenate(pieces, axis=2).reshape(rows, KCH)
        m_prev = m_scr[...]
        m_new = jnp.maximum(m_prev, jnp.max(sm, axis=1, keepdims=True))
        p = jnp.exp(sm - m_new)
        alpha = jnp.exp(m_prev - m_new)
        acc_scr[...] = alpha * acc_scr[...] + _nn(p.astype(BF16), av_ref[pl.ds(base, KCH), :])
        m_scr[...] = m_new
        return tie_seen

    lax.fori_loop(0, nch, attn_chunk, jnp.zeros((QB, LANES), F32))

    acc = acc_scr[...]
    norm = acc / pltpu.roll(acc, HEAD_DIM, axis=1)
    lo = lane < HEAD_DIM
    for cg in range(4):
        ev = norm[(2 * cg) * QB:(2 * cg + 1) * QB]
        od = norm[(2 * cg + 1) * QB:(2 * cg + 2) * QB]
        o_ref[:, cg * LANES:(cg + 1) * LANES] = jnp.where(
            lo, ev, pltpu.roll(od, HEAD_DIM, axis=1)).astype(BF16)


def _dsa(aq, iq, iw, ak, av, ik, tri, batch, seq):
    nqb = seq // QB
    top_k = min(A_TOPK_MAX, seq // 4)
    sspec = pl.BlockSpec((1, 8, QB, LANES), lambda b, i: (b * nqb + i, 0, 0, 0))
    kspec = pl.BlockSpec((seq, LANES), lambda b, i: (b, 0))
    return pl.pallas_call(
        functools.partial(_dsa_body, top_k=top_k),
        grid=(batch, nqb),
        in_specs=[sspec, sspec, pl.BlockSpec((QB, LANES), lambda b, i: (b * nqb + i, 0)),
                  kspec, kspec, kspec, pl.BlockSpec((LANES, LANES), lambda b, i: (0, 0))],
        out_specs=pl.BlockSpec((QB, A_HEADS * HEAD_DIM), lambda b, i: (b * nqb + i, 0)),
        out_shape=jax.ShapeDtypeStruct((batch * seq, A_HEADS * HEAD_DIM), BF16),
        scratch_shapes=[
            pltpu.VMEM((seq // LANES, QB, LANES), I32),
            pltpu.VMEM((IDX_HEADS, QB, LANES), F32),
            pltpu.VMEM((A_HEADS * QB, 1), F32),
            pltpu.VMEM((A_HEADS * QB, LANES), F32),
        ],
        compiler_params=_cparams(("parallel", "arbitrary")),
        name="dsa",
    )(aq, iq, iw, ak, av, ik, tri)


def _nsa_body(bq_ref, kc_ref, vc_ref, sk_ref, sv_ref, wk_ref, wv_ref, gt_ref,
              ov_ref, ex_ref, tri_ref, o_ref, m_scr, acc_scr):
    i = pl.program_id(1)
    q0 = i * QB
    nch = (q0 + QB - 1) // KCH + 1
    rows = B_HPG * QB
    ncmp = kc_ref.shape[0]
    row_t = q0 + lax.broadcasted_iota(I32, (QB, LANES), 0)
    lane = lax.broadcasted_iota(I32, (QB, LANES), 1)
    lo = lane < HEAD_DIM
    gates = gt_ref[...]
    tri = tri_ref[...]

    heads = [None] * B_HEADS
    for g in range(B_GROUPS):
        q = bq_ref[0, g * B_HPG:(g + 1) * B_HPG].reshape(rows, LANES)

        s = _nt(q, kc_ref[...]).reshape(B_HPG, QB, ncmp)
        c_end = lax.broadcasted_iota(I32, (B_HPG, QB, ncmp), 2) * CMP_STRIDE + (CMP_BLOCK - 1)
        cmask = c_end <= q0 + lax.broadcasted_iota(I32, (B_HPG, QB, ncmp), 1)
        sm = jnp.where(cmask, s, NEG_INF)
        e = jnp.exp(sm - jnp.max(sm, axis=2, keepdims=True))
        p = jnp.where(cmask, e / jnp.sum(e, axis=2, keepdims=True), 0.0)
        pb = p.astype(BF16).reshape(rows, ncmp)
        o_c = _nn(pb, vc_ref[...])
        if g == 1:
            o_c = pltpu.roll(o_c, HEAD_DIM, axis=1)
        imp = jnp.sum(_nn(pb, ov_ref[...]).reshape(B_HPG, QB, NSEL), axis=0)

        forced = (lane == (row_t // SEL_BLOCK)) | (lane == 0)
        admissible = lane * SEL_BLOCK <= row_t
        imp = jnp.where(forced, FORCED_SCORE, jnp.where(admissible, imp, -1.0))
        key = _sort_key(imp)
        tau = _kth_largest_key(
            lambda cand: _row_total(jnp.where(key >= cand, 1.0, 0.0)), float(SEL_TOPN))
        gt = key > tau
        eq = key == tau
        need = float(SEL_TOPN) - _row_total(jnp.where(gt, 1.0, 0.0))
        rank = _nn(jnp.where(eq, 1.0, 0.0).astype(BF16), tri)
        selb = jnp.where(gt | (eq & (rank <= need)), 1.0, 0.0).astype(BF16)

        m_scr[...] = jnp.full(m_scr.shape, NEG_INF, F32)
        acc_scr[...] = jnp.zeros_like(acc_scr)

        def sel_chunk(c, carry):
            base = pl.multiple_of(c * KCH, KCH)
            sc = _nt(q, sk_ref[pl.ds(base, KCH), :]).reshape(B_HPG, QB, KCH)
            s_idx = base + lax.broadcasted_iota(I32, (QB, KCH), 1)
            t_q = q0 + lax.broadcasted_iota(I32, (QB, KCH), 0)
            msk = (_nn(selb, ex_ref[c]) > 0.5) & (s_idx <= t_q)
            scm = jnp.where(msk[None], sc, NEG_INF).reshape(rows, KCH)
            m_prev = m_scr[...]
            m_new = jnp.maximum(m_prev, jnp.max(scm, axis=1, keepdims=True))
            pp = jnp.exp(scm - m_new)
            alpha = jnp.exp(m_prev - m_new)
            acc_scr[...] = alpha * acc_scr[...] + _nn(pp.astype(BF16), sv_ref[g, pl.ds(base, KCH), :])
            m_scr[...] = m_new
            return carry

        lax.fori_loop(0, nch, sel_chunk, 0)
        acc = acc_scr[...]
        o_s = acc / pltpu.roll(acc, HEAD_DIM, axis=1)

        wlen = QB + WINDOW
        start = pl.multiple_of(jnp.maximum(q0 - WINDOW, 0), LANES)
        sw = _nt(q, wk_ref[pl.ds(start, wlen), :]).reshape(B_HPG, QB, wlen)
        w_idx = start + lax.broadcasted_iota(I32, (B_HPG, QB, wlen), 2)
        t_w = q0 + lax.broadcasted_iota(I32, (B_HPG, QB, wlen), 1)
        wmask = (w_idx <= t_w) & (w_idx > t_w - WINDOW)
        swm = jnp.where(wmask, sw, NEG_INF)
        pw = jnp.exp(swm - jnp.max(swm, axis=2, keepdims=True)).reshape(rows, wlen)
        accw = _nn(pw.astype(BF16), wv_ref[g, pl.ds(start, wlen), :])
        o_w = accw / pltpu.roll(accw, HEAD_DIM, axis=1)

        for hh in range(B_HPG):
            h = g * B_HPG + hh
            rs = slice(hh * QB, (hh + 1) * QB)
            col = h * N_BRANCH_GATES
            heads[h] = (gates[:, col:col + 1] * o_c[rs] + gates[:, col + 1:col + 2] * o_s[rs]
                        + gates[:, col + 2:col + 3] * o_w[rs])

    for cg in range(4):
        o_ref[:, cg * LANES:(cg + 1) * LANES] = jnp.where(
            lo, heads[2 * cg], pltpu.roll(heads[2 * cg + 1], HEAD_DIM, axis=1)).astype(BF16)


def _nsa(bq, kc, vc, sk, sv, wk, wv, gt, ov, ex, tri, batch, seq):
    nqb = seq // QB
    ncmp = seq // CMP_STRIDE
    sspec = pl.BlockSpec((1, 8, QB, LANES), lambda b, i: (b * nqb + i, 0, 0, 0))
    cspec = pl.BlockSpec((ncmp, LANES), lambda b, i: (b, 0))
    kspec = pl.BlockSpec((seq, LANES), lambda b, i: (b, 0))
    vspec = pl.BlockSpec((2, seq, LANES), lambda b, i: (0, b, 0))
    return pl.pallas_call(
        _nsa_body,
        grid=(batch, nqb),
        in_specs=[sspec, cspec, cspec, kspec, vspec, kspec, vspec,
                  pl.BlockSpec((QB, LANES), lambda b, i: (b * nqb + i, 0)),
                  pl.BlockSpec((ncmp, NSEL), lambda b, i: (0, 0)),
                  pl.BlockSpec((seq // KCH, NSEL, KCH), lambda b, i: (0, 0, 0)),
                  pl.BlockSpec((LANES, LANES), lambda b, i: (0, 0))],
        out_specs=pl.BlockSpec((QB, B_HEADS * HEAD_DIM), lambda b, i: (b * nqb + i, 0)),
        out_shape=jax.ShapeDtypeStruct((batch * seq, B_HEADS * HEAD_DIM), BF16),
        scratch_shapes=[pltpu.VMEM((B_HPG * QB, 1), F32),
                        pltpu.VMEM((B_HPG * QB, LANES), F32)],
        compiler_params=_cparams(("parallel", "arbitrary")),
        name="nsa",
    )(bq, kc, vc, sk, sv, wk, wv, gt, ov, ex, tri)


def _merge_body(x_ref, oa_ref, ob_ref, gm_ref, wpa_ref, wpb_ref, wo_ref, o_ref):
    gm = gm_ref[...]
    ga = _sigmoid(gm[:, :D_MODEL])
    gb = _sigmoid(gm[:, D_MODEL:])
    merged = ga * _nn(oa_ref[...], wpa_ref[...]) + gb * _nn(ob_ref[...], wpb_ref[...])
    o_ref[...] = x_ref[...] + _nn(merged.astype(BF16), wo_ref[...])


def _merge(x, oa, ob, z, wpa, wpb, wo, tm=512):
    m, d = x.shape
    full = lambda a: pl.BlockSpec(a.shape, lambda i: (0, 0))
    return pl.pallas_call(
        _merge_body,
        grid=(m // tm,),
        in_specs=[pl.BlockSpec((tm, d), lambda i: (i, 0)),
                  pl.BlockSpec((tm, oa.shape[1]), lambda i: (i, 0)),
                  pl.BlockSpec((tm, ob.shape[1]), lambda i: (i, 0)),
                  pl.BlockSpec((tm, 2 * d), lambda i: (i, OFF_GM // (2 * d))),
                  full(wpa), full(wpb), full(wo)],
        out_specs=pl.BlockSpec((tm, d), lambda i: (i, 0)),
        out_shape=jax.ShapeDtypeStruct((m, d), F32),
        compiler_params=_cparams(("parallel",)),
        name="merge",
    )(x, oa, ob, z, wpa, wpb, wo)


def _ple_body(x_ref, p_ref, g_ref, wg_ref, wp_ref, fn_ref, o_ref, *, final):
    x = x_ref[...]
    gate = _sigmoid(_nn(_rms(x, g_ref[...]).astype(BF16), wg_ref[...]))
    y = x + gate * _nn(p_ref[...].astype(BF16), wp_ref[...])
    o_ref[...] = _rms(y, fn_ref[...]) if final else y


def _ple(x, p, g, wg, wp, fn, final, tm=512):
    m, d = x.shape
    full = lambda a: pl.BlockSpec(a.shape, lambda i: (0, 0))
    return pl.pallas_call(
        functools.partial(_ple_body, final=final),
        grid=(m // tm,),
        in_specs=[pl.BlockSpec((tm, d), lambda i: (i, 0)),
                  pl.BlockSpec((tm, p.shape[1]), lambda i: (i, 0)),
                  full(g), full(wg), full(wp), full(fn)],
        out_specs=pl.BlockSpec((tm, d), lambda i: (i, 0)),
        out_shape=jax.ShapeDtypeStruct((m, d), F32),
        compiler_params=_cparams(("parallel",)),
        name="ple",
    )(x, p, g, wg, wp, fn)


def _pad_w_in(w_in):
    sizes = (512, 128, 256, 32, 8, 512, 768, 24, 2048)
    offs = (OFF_QA, OFF_CKV, OFF_QI, OFF_KI, OFF_WI, OFF_QB, OFF_KVB, OFF_GB, OFF_GM)
    out = jnp.zeros(w_in.shape[:2] + (Z_COLS,), BF16)
    src = 0
    for size, off in zip(sizes, offs):
        out = out.at[:, :, off:off + size].set(w_in[:, :, src:src + size].astype(BF16))
        src += size
    return out


def _cmp_w1(w1):
    nl = w1.shape[0]
    w = w1.reshape(nl, 2, CMP_STRIDE, HEAD_DIM, CMP_HIDDEN).astype(BF16)
    out = jnp.zeros((nl, B_GROUPS, 2, CMP_STRIDE, B_GROUPS, HEAD_DIM, CMP_HIDDEN), BF16)
    for g in range(B_GROUPS):
        out = out.at[:, g, :, :, g].set(w)
    return out.reshape(nl, B_GROUPS, 2, CMP_STRIDE * LANES, CMP_HIDDEN)


def _cmp_w2(w2):
    nl = w2.shape[0]
    out = jnp.zeros((nl, B_GROUPS, CMP_HIDDEN, B_GROUPS, HEAD_DIM), BF16)
    for g in range(B_GROUPS):
        out = out.at[:, g, :, g].set(w2.astype(BF16))
    return out.reshape(nl, B_GROUPS, CMP_HIDDEN, LANES)


def _cmp_pos(cmp_pos):
    nl = cmp_pos.shape[0]
    p = cmp_pos.reshape(nl, 2, CMP_STRIDE, 1, HEAD_DIM)
    return jnp.broadcast_to(p, (nl, 2, CMP_STRIDE, B_GROUPS, HEAD_DIM)).reshape(nl, 2, CMP_STRIDE * LANES)


def _masks(seq):
    ncmp = seq // CMP_STRIDE
    c = np.arange(ncmp)[:, None] * CMP_STRIDE
    n = np.arange(NSEL)[None, :] * SEL_BLOCK
    n_real = (seq - CMP_BLOCK) // CMP_STRIDE + 1
    overlap = (c < n + SEL_BLOCK) & (n < c + CMP_BLOCK) & (np.arange(ncmp)[:, None] < n_real)
    s = np.arange(seq).reshape(seq // KCH, 1, KCH)
    expand = (s // SEL_BLOCK) == np.arange(NSEL)[None, :, None]
    tri = np.arange(LANES)[:, None] <= np.arange(LANES)[None, :]
    return (jnp.asarray(overlap, BF16), jnp.asarray(expand, BF16), jnp.asarray(tri, BF16))


def kernel(x, p, ffa_norm, ffa_w1, ffa_w3, ffa_w2, mix_norm, w_in, a_kv_norm, a_w_ukv, cmp_pos, cmp_k_w1, cmp_k_w2, cmp_v_w1, cmp_v_w2, w_proj_a, w_proj_b, w_out, ffb_norm, ffb_w1, ffb_w3, ffb_w2, ple_norm, ple_w_gate, ple_w_proj, final_norm):
    batch, seq, d = x.shape
    depth = w_in.shape[0]
    m = batch * seq
    bf = lambda a: a.astype(BF16)
    ffa_w1, ffa_w3, ffa_w2 = bf(ffa_w1), bf(ffa_w3), bf(ffa_w2)
    ffb_w1, ffb_w3, ffb_w2 = bf(ffb_w1), bf(ffb_w3), bf(ffb_w2)
    w_in_p = _pad_w_in(w_in)
    ukv = bf(a_w_ukv)
    wk1, wk2 = _cmp_w1(cmp_k_w1), _cmp_w2(cmp_k_w2)
    wv1, wv2 = _cmp_w1(cmp_v_w1), _cmp_w2(cmp_v_w2)
    pos2 = _cmp_pos(cmp_pos)
    wpa, wpb, wo = bf(w_proj_a), bf(w_proj_b), bf(w_out)
    wg, wp = bf(ple_w_gate), bf(ple_w_proj)
    tabs = _rope_tables(seq)
    overlap, expand, tri = _masks(seq)
    fn = final_norm.reshape(1, d)

    xf = x.reshape(m, d)
    pf = p.reshape(depth, m, p.shape[-1])
    for i in range(depth):
        xf = _ffn(xf, ffa_norm[i].reshape(1, d), ffa_w1[i], ffa_w3[i], ffa_w2[i])
        z = _inproj(xf, mix_norm[i].reshape(1, d), w_in_p[i])
        (aq, ak, av, iq, ik, iw, bq, ck, cv, sk, sv, wk, wv, gt) = _prep(
            z, tabs, a_kv_norm[i].reshape(1, A_KV_RANK), ukv[i], seq)
        kc, vc = _compress(ck, cv, pos2[i], wk1[i], wk2[i], wv1[i], wv2[i], batch, seq)
        oa = _dsa(aq, iq, iw, ak, av, ik, tri, batch, seq)
        ob = _nsa(bq, kc, vc, sk, sv, wk, wv, gt, overlap, expand, tri, batch, seq)
        xf = _merge(xf, oa, ob, z, wpa[i], wpb[i], wo[i])
        xf = _ffn(xf, ffb_norm[i].reshape(1, d), ffb_w1[i], ffb_w3[i], ffb_w2[i])
        xf = _ple(xf, pf[i], ple_norm[i].reshape(1, d), wg[i], wp[i], fn, final=(i == depth - 1))
    return xf.reshape(batch, seq, d)
```

```python
import functools

import numpy as np
import jax
import jax.numpy as jnp
from jax import lax
from jax.experimental import pallas as pl
from jax.experimental.pallas import tpu as pltpu

F32 = jnp.float32
BF16 = jnp.bfloat16
I32 = jnp.int32

D_MODEL = 1024
DEPTH = 4
PLE_DIM = 256
D_FF = 2816
ROPE_THETA = 500000.0
HEAD_DIM = 64
ROT_DIM = HEAD_DIM // 4
NEG_INF = -1e30
RMS_EPS = 1e-6
A_HEADS = 8
A_KV_RANK = 128
IDX_HEADS = 8
IDX_DIM = 32
IDX_ROT_DIM = IDX_DIM // 4
A_TOPK_MAX = 256
B_HEADS = 8
B_GROUPS = 2
B_HPG = B_HEADS // B_GROUPS
CMP_BLOCK = 32
CMP_STRIDE = 16
CMP_HIDDEN = 256
SEL_BLOCK = 64
SEL_TOPN = 16
WINDOW = 512
FORCED_SCORE = 1e4
N_BRANCH_GATES = 3

LANES = 128
QB = 128
KCH = 512
NSEL = 128
GATE_ROWS = 32
SEL_SHIFT = SEL_BLOCK.bit_length() - 1
assert 1 << SEL_SHIFT == SEL_BLOCK
PACK = 16
INT_MIN = -2147483648
KEY_NEG_INF = -2139095041
I16 = jnp.int16

OFF_QA, OFF_QB, OFF_KVB, OFF_GM = 0, 512, 1024, 2048
OFF_QI, OFF_CKV, OFF_KI, OFF_WI, OFF_GB = 4096, 4352, 4480, 4608, 4736
Z_COLS = 4864
VMEM_LIMIT = 56 * 1024 * 1024


def _nt(a, b):
    return lax.dot_general(a, b, (((1,), (1,)), ((), ())), preferred_element_type=F32)


def _nn(a, b):
    return jnp.dot(a, b, preferred_element_type=F32)


def _cparams(sem):
    return pltpu.CompilerParams(dimension_semantics=sem, vmem_limit_bytes=VMEM_LIMIT)


def _rms(x, g):
    ms = jnp.mean(x * x, axis=-1, keepdims=True)
    return x * lax.rsqrt(ms + RMS_EPS) * g


def _sigmoid(x):
    return 1.0 / (1.0 + jnp.exp(-x))


def _ffn_body(x_ref, g_ref, w1_ref, w3_ref, w2_ref, o_ref, h_scr, acc_scr):
    j = pl.program_id(1)

    @pl.when(j == 0)
    def _():
        h_scr[...] = _rms(x_ref[...], g_ref[...]).astype(BF16)
        acc_scr[...] = jnp.zeros_like(acc_scr)

    h = h_scr[...]
    a = _nn(h, w1_ref[...])
    b = _nn(h, w3_ref[...])
    u = (a * _sigmoid(a)) * b
    acc_scr[...] += _nn(u.astype(BF16), w2_ref[...])

    @pl.when(j == pl.num_programs(1) - 1)
    def _():
        o_ref[...] = x_ref[...] + 0.5 * acc_scr[...]


def _ffn(x, g, w1, w3, w2, tm=512, tf=1408):
    m, d = x.shape
    f = w1.shape[1]
    return pl.pallas_call(
        _ffn_body,
        grid=(m // tm, f // tf),
        in_specs=[
            pl.BlockSpec((tm, d), lambda i, j: (i, 0)),
            pl.BlockSpec((1, d), lambda i, j: (0, 0)),
            pl.BlockSpec((d, tf), lambda i, j: (0, j)),
            pl.BlockSpec((d, tf), lambda i, j: (0, j)),
            pl.BlockSpec((tf, d), lambda i, j: (j, 0)),
        ],
        out_specs=pl.BlockSpec((tm, d), lambda i, j: (i, 0)),
        out_shape=jax.ShapeDtypeStruct((m, d), F32),
        scratch_shapes=[pltpu.VMEM((tm, d), BF16), pltpu.VMEM((tm, d), F32)],
        compiler_params=_cparams(("parallel", "arbitrary")),
        name="ffn",
    )(x, g, w1, w3, w2)


def _inproj_body(x_ref, g_ref, w_ref, o_ref, h_scr):
    @pl.when(pl.program_id(1) == 0)
    def _():
        h_scr[...] = _rms(x_ref[...], g_ref[...]).astype(BF16)

    o_ref[...] = _nn(h_scr[...], w_ref[...])


def _inproj(x, g, w, tm=512, tn=2432):
    m, d = x.shape
    n = w.shape[1]
    return pl.pallas_call(
        _inproj_body,
        grid=(m // tm, n // tn),
        in_specs=[
            pl.BlockSpec((tm, d), lambda i, j: (i, 0)),
            pl.BlockSpec((1, d), lambda i, j: (0, 0)),
            pl.BlockSpec((d, tn), lambda i, j: (0, j)),
        ],
        out_specs=pl.BlockSpec((tm, tn), lambda i, j: (i, j)),
        out_shape=jax.ShapeDtypeStruct((m, n), F32),
        scratch_shapes=[pltpu.VMEM((tm, d), BF16)],
        compiler_params=_cparams(("parallel", "arbitrary")),
        name="inproj",
    )(x, g, w)


def _rope_tables(seq):
    pos = jnp.arange(seq, dtype=F32)
    lane = np.arange(LANES)
    out = []
    for rot_dim, period, first_only in ((ROT_DIM, HEAD_DIM, False),
                                        (ROT_DIM, HEAD_DIM, True),
                                        (IDX_ROT_DIM, IDX_DIM, False)):
        half = rot_dim // 2
        inv = ROPE_THETA ** (-jnp.arange(0, rot_dim, 2, dtype=F32) / rot_dim)
        ang = pos[:, None] * inv[None, :]
        cos, sin = jnp.cos(ang), jnp.sin(ang)
        lp = lane % period
        in1 = lp < half
        in2 = (lp >= half) & (lp < 2 * half)
        if first_only:
            in1 = in1 & (lane < period)
            in2 = in2 & (lane < period)
        fidx = np.where(in1, lp, np.where(in2, lp - half, 0))
        cos_l, sin_l = cos[:, fidx], sin[:, fidx]
        out.append(jnp.where((in1 | in2)[None, :], cos_l, 1.0))
        out.append(jnp.where(in1[None, :], -sin_l, 0.0))
        out.append(jnp.where(in2[None, :], sin_l, 0.0))
    return jnp.stack(out).astype(F32)


def _rope(x, tab_ref, k, half):
    a, b, c = tab_ref[3 * k], tab_ref[3 * k + 1], tab_ref[3 * k + 2]
    return (x * a + pltpu.roll(x, LANES - half, axis=1) * b
            + pltpu.roll(x, half, axis=1) * c)


def _prep_body(qa_ref, qb_ref, kvb_ref, qi_ref, ckv_ref, ki_ref, wi_ref, gb_ref,
               tab_ref, kvn_ref, ukv_ref,
               aq_ref, ak_ref, av_ref, iq_ref, ik_ref, iw_ref,
               bq_ref, ck_ref, cv_ref, sk_ref, sv_ref, wk_ref, wv_ref, gt_ref):
    tr = qa_ref.shape[0]
    nq = tr // QB
    lane = lax.broadcasted_iota(I32, (tr, LANES), 1)
    lo = lane < HEAD_DIM
    scale = HEAD_DIM ** -0.5
    half = ROT_DIM // 2

    for cg in range(4):
        sl = slice(cg * LANES, (cg + 1) * LANES)
        ya = _rope(qa_ref[:, sl], tab_ref, 0, half) * scale
        ya_sw = pltpu.roll(ya, HEAD_DIM, axis=1)
        even = jnp.where(lo, ya, 0.0).astype(BF16)
        odd = jnp.where(lo, ya_sw, 0.0).astype(BF16)
        yb = _rope(qb_ref[:, sl], tab_ref, 0, half) * scale
        yb_sw = pltpu.roll(yb, HEAD_DIM, axis=1)
        if cg < 2:
            b_even = jnp.where(lo, yb, 0.0).astype(BF16)
            b_odd = jnp.where(lo, yb_sw, 0.0).astype(BF16)
        else:
            b_even = jnp.where(lo, 0.0, yb_sw).astype(BF16)
            b_odd = jnp.where(lo, 0.0, yb).astype(BF16)
        for q in range(nq):
            rows = slice(q * QB, (q + 1) * QB)
            aq_ref[q, 2 * cg] = even[rows]
            aq_ref[q, 2 * cg + 1] = odd[rows]
            bq_ref[q, 2 * cg] = b_even[rows]
            bq_ref[q, 2 * cg + 1] = b_odd[rows]

    first = lane < IDX_DIM
    for cg in range(2):
        y = _rope(qi_ref[:, cg * LANES:(cg + 1) * LANES], tab_ref, 2, IDX_ROT_DIM // 2)
        for k in range(4):
            yk = y if k == 0 else pltpu.roll(y, LANES - IDX_DIM * k, axis=1)
            piece = jnp.where(first, yk, 0.0).astype(BF16)
            for q in range(nq):
                iq_ref[q, 4 * cg + k] = piece[q * QB:(q + 1) * QB]

    ik_ref[...] = _rope(ki_ref[...], tab_ref, 2, IDX_ROT_DIM // 2).astype(BF16)
    iw_ref[...] = jnp.transpose(wi_ref[...] * (IDX_HEADS ** -0.5))[0:IDX_HEADS]

    def vt(v):
        return jnp.transpose(jnp.where(lo, v, 1.0)).astype(BF16)

    c = _rms(ckv_ref[...], kvn_ref[...]).astype(BF16)
    kv = _rope(_nn(c, ukv_ref[...]), tab_ref, 1, half)
    ak_ref[...] = jnp.where(lo, kv, 0.0).astype(BF16)
    av_ref[0] = vt(pltpu.roll(kv, HEAD_DIM, axis=1))

    ck_ref[...] = _rope(kvb_ref[:, 0:128], tab_ref, 0, half)
    cv_ref[...] = kvb_ref[:, 128:256]
    sk_ref[...] = _rope(kvb_ref[:, 256:384], tab_ref, 0, half).astype(BF16)
    vs = kvb_ref[:, 384:512]
    wk_ref[...] = _rope(kvb_ref[:, 512:640], tab_ref, 0, half).astype(BF16)
    vw = kvb_ref[:, 640:768]
    sv_ref[0, 0] = vt(vs)
    sv_ref[1, 0] = vt(pltpu.roll(vs, HEAD_DIM, axis=1))
    for g, v in enumerate((vw, pltpu.roll(vw, HEAD_DIM, axis=1))):
        t = vt(v)
        for j in range(tr // LANES):
            wv_ref[g, j] = t[:, j * LANES:(j + 1) * LANES]
    gt_ref[...] = jnp.transpose(_sigmoid(gb_ref[...]))[0:GATE_ROWS]


def _prep(z, tabs, kvn, ukv, seq):
    tr = KCH
    m = z.shape[0]
    nq = tr // QB
    nqb_total = m // QB
    spt = seq // tr

    def zspec(width, off):
        return pl.BlockSpec((tr, width), lambda i, o=off // width: (i, o))

    row = lambda dt: jax.ShapeDtypeStruct((m, LANES), dt)
    rspec = pl.BlockSpec((tr, LANES), lambda i: (i, 0))
    stack = jax.ShapeDtypeStruct((nqb_total, 8, QB, LANES), BF16)
    sspec = pl.BlockSpec((nq, 8, QB, LANES), lambda i: (i, 0, 0, 0))
    colT = lambda r: (jax.ShapeDtypeStruct((r, m), F32), pl.BlockSpec((r, tr), lambda i: (0, i)))
    iw_shape, iw_spec = colT(IDX_HEADS)
    gt_shape, gt_spec = colT(GATE_ROWS)
    av_shape = jax.ShapeDtypeStruct((m // tr, LANES, tr), BF16)
    av_spec = pl.BlockSpec((1, LANES, tr), lambda i: (i, 0, 0))
    sv_shape = jax.ShapeDtypeStruct((B_GROUPS, m // tr, LANES, tr), BF16)
    sv_spec = pl.BlockSpec((B_GROUPS, 1, LANES, tr), lambda i: (0, i, 0, 0))
    wv_shape = jax.ShapeDtypeStruct((B_GROUPS, m // LANES, LANES, LANES), BF16)
    wv_spec = pl.BlockSpec((B_GROUPS, tr // LANES, LANES, LANES), lambda i: (0, i, 0, 0))
    return pl.pallas_call(
        _prep_body,
        grid=(m // tr,),
        in_specs=[
            zspec(512, OFF_QA), zspec(512, OFF_QB), zspec(1024, OFF_KVB),
            zspec(256, OFF_QI), zspec(128, OFF_CKV), zspec(128, OFF_KI),
            zspec(128, OFF_WI), zspec(128, OFF_GB),
            pl.BlockSpec((9, tr, LANES), lambda i: (0, i % spt, 0)),
            pl.BlockSpec((1, LANES), lambda i: (0, 0)),
            pl.BlockSpec((LANES, LANES), lambda i: (0, 0)),
        ],
        out_specs=[sspec, rspec, av_spec, sspec, rspec, iw_spec,
                   sspec, rspec, rspec, rspec, sv_spec, rspec, wv_spec, gt_spec],
        out_shape=[stack, row(BF16), av_shape, stack, row(BF16), iw_shape,
                   stack, row(F32), row(F32), row(BF16), sv_shape, row(BF16), wv_shape, gt_shape],
        compiler_params=_cparams(("parallel",)),
        name="prep",
    )(z, z, z, z, z, z, z, z, tabs, kvn, ukv)


def _compress_body(xk_ref, xv_ref, pos_ref, wk1_ref, wk2_ref, wv1_ref, wv2_ref, kc_ref, vc_ref):
    nh = xk_ref.shape[0]

    def one(x_ref, w1_ref, w2_ref):
        x = x_ref[...]
        top = (x + pos_ref[0:1, :]).astype(BF16)
        bot = (x + pos_ref[1:2, :]).astype(BF16)
        out = jnp.zeros((nh, LANES), F32)
        for g in range(B_GROUPS):
            a = _nn(top, w1_ref[g, 0])
            b = _nn(bot, w1_ref[g, 1])
            hid = a + pltpu.roll(b, nh - 1, axis=0)
            act = (hid * _sigmoid(hid)).astype(BF16)
            out = out + _nn(act, w2_ref[g])
        return out

    kc_ref[...] = one(xk_ref, wk1_ref, wk2_ref).astype(BF16)
    vc_ref[...] = jnp.transpose(one(xv_ref, wv1_ref, wv2_ref)).astype(BF16)


def _compress(ck, cv, pos2, wk1, wk2, wv1, wv2, batch, seq):
    nh = seq // CMP_STRIDE
    width = CMP_STRIDE * LANES
    xk = ck.reshape(batch * nh, width)
    xv = cv.reshape(batch * nh, width)
    xspec = pl.BlockSpec((nh, width), lambda b: (b, 0))
    w1spec = pl.BlockSpec((B_GROUPS, 2, width, CMP_HIDDEN), lambda b: (0, 0, 0, 0))
    w2spec = pl.BlockSpec((B_GROUPS, CMP_HIDDEN, LANES), lambda b: (0, 0, 0))
    return pl.pallas_call(
        _compress_body,
        grid=(batch,),
        in_specs=[xspec, xspec, pl.BlockSpec((2, width), lambda b: (0, 0)),
                  w1spec, w2spec, w1spec, w2spec],
        out_specs=[pl.BlockSpec((nh, LANES), lambda b: (b, 0)),
                   pl.BlockSpec((LANES, nh), lambda b: (b, 0))],
        out_shape=[jax.ShapeDtypeStruct((batch * nh, LANES), BF16),
                   jax.ShapeDtypeStruct((batch * LANES, nh), BF16)],
        compiler_params=_cparams(("parallel",)),
        name="compress",
    )(xk, xv, pos2, wk1, wk2, wv1, wv2)


def _sort_key(x):
    x = jnp.where(x == 0.0, 0.0, x)
    bits = lax.bitcast_convert_type(x, I32)
    return bits ^ ((bits >> 31) & 0x7FFFFFFF)


def _col_sum(x):
    return jnp.sum(x, axis=0, keepdims=True)


def _bisect(count_ge, k, nbits, lowest):
    def step(it, tau):
        cand = tau + lax.shift_left(jnp.int32(1), nbits - 1 - it)
        return jnp.where(count_ge(cand) >= k, cand, tau)

    return lax.fori_loop(0, nbits, step, jnp.full((1, QB), lowest, I32))


def _pack_row(v):
    return jnp.broadcast_to(v, (PACK, QB)).astype(I16)[None]


def _count16(scr, nch, pred):
    def body(c, cnt):
        v = scr[c].reshape(KCH // PACK, PACK, QB)
        hit = jnp.where(pred(v), jnp.int16(1), jnp.int16(0))
        return cnt + _tree_sum([hit[k] for k in range(KCH // PACK)])

    cnt = lax.fori_loop(0, nch, body, jnp.zeros((PACK, QB), I16))
    return _col_sum(cnt.astype(I32))


def _tree_sum(xs):
    while len(xs) > 1:
        xs = [xs[k] + xs[k + 1] for k in range(0, len(xs) - 1, 2)] + ([xs[-1]] if len(xs) % 2 else [])
    return xs[0]


def _online_softmax_step(chunks, q, heads, m_scr, acc_scr):
    scores = [_nt(k_chunk, q).astype(BF16) for k_chunk, _, _ in chunks]
    m_prev = m_scr[...]
    acc = acc_scr[...]
    for (_, sel, v_t), s in zip(chunks, scores):
        sel16 = jnp.where(sel, 1.0, 0.0).astype(BF16) > 0.5
        sm = jnp.concatenate(
            [jnp.where(sel16, s[:, h * QB:(h + 1) * QB], NEG_INF) for h in range(heads)], axis=1)
        m_new = jnp.maximum(m_prev, jnp.max(sm, axis=0, keepdims=True).astype(F32))
        p = jnp.exp(sm - m_new.astype(BF16))
        acc = jnp.exp(m_prev - m_new) * acc + _nn(v_t, p)
        m_prev = m_new
    m_scr[...] = m_prev
    acc_scr[...] = acc


def _for_chunk_pairs(nch, step, init):
    carry = lax.fori_loop(0, nch // 2, lambda c2, cr: step([2 * c2, 2 * c2 + 1], cr), init)

    @pl.when(nch % 2 == 1)
    def _():
        step([nch - 1], carry)


def _store_heads(o_ref, heads_t):
    for cg in range(len(heads_t) // 2):
        blk = jnp.concatenate([heads_t[2 * cg], heads_t[2 * cg + 1]], axis=0)
        o_ref[:, cg * LANES:(cg + 1) * LANES] = jnp.transpose(blk).astype(BF16)


def _dsa_body(aq_ref, iq_ref, iw_ref, ak_ref, av_ref, ik_ref, tril_ref, o_ref,
              key_scr, hi_scr, lo_scr, m_scr, acc_scr, *, top_k):
    i = pl.program_id(1)
    q0 = i * QB
    nch = i // (KCH // QB) + 1
    cols = A_HEADS * QB
    k_pos = lax.broadcasted_iota(I32, (KCH, QB), 0)
    q_pos = q0 + lax.broadcasted_iota(I32, (KCH, QB), 1)
    iw = iw_ref[...]
    iq = iq_ref[0].reshape(cols, LANES)

    def score_chunk(c, carry):
        base = pl.multiple_of(c * KCH, KCH)
        r = _nt(ik_ref[pl.ds(base, KCH), :], iq)
        sc = jnp.zeros((KCH, QB), F32)
        for h in range(IDX_HEADS):
            sc = sc + jnp.maximum(r[:, h * QB:(h + 1) * QB], 0.0) * iw[h:h + 1, :]
        sc = jnp.where(base + k_pos <= q_pos, sc, -jnp.inf)
        key = _sort_key(sc)
        key_scr[c] = key
        hi_scr[c] = (key >> 16).astype(I16)
        return carry

    lax.fori_loop(0, nch, score_chunk, 0)

    def kth16(scr, k):
        return _bisect(
            lambda cand: _count16(scr, nch, lambda v, c=_pack_row(cand): v >= c), k, 16, -32768)

    t_hi = kth16(hi_scr, top_k)
    t_hi_p = _pack_row(t_hi)
    n_gt_hi = _count16(hi_scr, nch, lambda v: v > t_hi_p)

    def fill_lo(c, carry):
        lo = ((key_scr[c] & 0xFFFF) - 32768).astype(I16).reshape(KCH // PACK, PACK, QB)
        in_bucket = hi_scr[c].reshape(KCH // PACK, PACK, QB) == t_hi_p
        lo_scr[c] = jnp.where(in_bucket, lo, jnp.int16(-32768)).reshape(KCH, QB)
        return carry

    lax.fori_loop(0, nch, fill_lo, 0)
    t_lo = kth16(lo_scr, top_k - n_gt_hi)
    t_lo_p = _pack_row(t_lo)
    tau = t_hi * 65536 + (t_lo + 32768)
    n_gt = n_gt_hi + _count16(lo_scr, nch, lambda v: v > t_lo_p)
    n_eq = _count16(lo_scr, nch, lambda v: v == t_lo_p)
    need = top_k - n_gt
    surplus = jnp.max(jnp.where((n_eq > need) & (tau > KEY_NEG_INF), 1, 0))
    need_f = need.astype(F32)

    aq = aq_ref[0].reshape(cols, LANES)
    m_scr[...] = jnp.full(m_scr.shape, NEG_INF, F32)
    acc_scr[...] = jnp.zeros_like(acc_scr)

    def chunk(c, sel):
        base = pl.multiple_of(c * KCH, KCH)
        return ak_ref[pl.ds(base, KCH), :], sel & (base + k_pos <= q_pos), av_ref[c]

    @pl.when(surplus == 0)
    def _():
        def step(cs, carry):
            _online_softmax_step([chunk(c, key_scr[c] >= tau) for c in cs],
                                 aq, A_HEADS, m_scr, acc_scr)
            return carry

        _for_chunk_pairs(nch, step, 0)

    @pl.when(surplus != 0)
    def _():
        def step(cs, tie_seen):
            chunks = []
            for c in cs:
                key = key_scr[c]
                eq = key == tau
                eqf = jnp.where(eq, 1.0, 0.0)
                rank = tie_seen + _nn(tril_ref[...], eqf.astype(BF16))
                chunks.append(chunk(c, (key > tau) | (eq & (rank <= need_f))))
                tie_seen = tie_seen + _col_sum(eqf)
            _online_softmax_step(chunks, aq, A_HEADS, m_scr, acc_scr)
            return tie_seen

        _for_chunk_pairs(nch, step, jnp.zeros((1, QB), F32))

    acc = acc_scr[...]
    out = acc[0:HEAD_DIM] / acc[HEAD_DIM:HEAD_DIM + 1]
    _store_heads(o_ref, [out[:, h * QB:(h + 1) * QB] for h in range(A_HEADS)])


def _dsa(aq, iq, iw, ak, av, ik, tril, batch, seq):
    nqb = seq // QB
    nkc = seq // KCH
    top_k = min(A_TOPK_MAX, seq // 4)
    sspec = pl.BlockSpec((1, 8, QB, LANES), lambda b, i: (b * nqb + i, 0, 0, 0))
    kspec = pl.BlockSpec((seq, LANES), lambda b, i: (b, 0))
    return pl.pallas_call(
        functools.partial(_dsa_body, top_k=top_k),
        grid=(batch, nqb),
        in_specs=[sspec, sspec, pl.BlockSpec((IDX_HEADS, QB), lambda b, i: (0, b * nqb + i)),
                  kspec, pl.BlockSpec((nkc, LANES, KCH), lambda b, i: (b, 0, 0)), kspec,
                  pl.BlockSpec((KCH, KCH), lambda b, i: (0, 0))],
        out_specs=pl.BlockSpec((QB, A_HEADS * HEAD_DIM), lambda b, i: (b * nqb + i, 0)),
        out_shape=jax.ShapeDtypeStruct((batch * seq, A_HEADS * HEAD_DIM), BF16),
        scratch_shapes=[
            pltpu.VMEM((nkc, KCH, QB), I32),
            pltpu.VMEM((nkc, KCH, QB), I16),
            pltpu.VMEM((nkc, KCH, QB), I16),
            pltpu.VMEM((1, A_HEADS * QB), F32),
            pltpu.VMEM((LANES, A_HEADS * QB), F32),
        ],
        compiler_params=_cparams(("parallel", "arbitrary")),
        name="dsa",
    )(aq, iq, iw, ak, av, ik, tril)


def _nsa_body(bq_ref, kc_ref, vc_ref, sk_ref, sv_ref, wk_ref, wv_ref, gt_ref,
              ov_ref, ex_ref, tril_ref, o_ref, m_scr, acc_scr):
    i = pl.program_id(1)
    q0 = i * QB
    nch = i // (KCH // QB) + 1
    cols = B_HPG * QB
    ncmp = kc_ref.shape[0]
    gates = gt_ref[...]
    n_idx = lax.broadcasted_iota(I32, (NSEL, QB), 0)
    t_q = q0 + lax.broadcasted_iota(I32, (NSEL, QB), 1)
    k_pos = lax.broadcasted_iota(I32, (KCH, QB), 0)
    q_pos = q0 + lax.broadcasted_iota(I32, (KCH, QB), 1)

    def t_of_col(shape):
        return q0 + (lax.broadcasted_iota(I32, shape, 1) & (QB - 1))

    heads = [None] * B_HEADS
    for g in range(B_GROUPS):
        q = bq_ref[0, g * B_HPG:(g + 1) * B_HPG].reshape(cols, LANES)
        grp = slice(g * HEAD_DIM, (g + 1) * HEAD_DIM)

        s = _nt(kc_ref[...], q)
        c_end = lax.broadcasted_iota(I32, (ncmp, cols), 0) * CMP_STRIDE + (CMP_BLOCK - 1)
        cmask = c_end <= t_of_col((ncmp, cols))
        sm = jnp.where(cmask, s, NEG_INF)
        e = jnp.exp(sm - jnp.max(sm, axis=0, keepdims=True))
        pb = jnp.where(cmask, e / _col_sum(e), 0.0).astype(BF16)
        o_c = _nn(vc_ref[...], pb)[grp]
        imp_h = _nn(ov_ref[...], pb)
        imp = imp_h[:, 0:QB]
        for hh in range(1, B_HPG):
            imp = imp + imp_h[:, hh * QB:(hh + 1) * QB]

        forced = (n_idx == lax.shift_right_logical(t_q, SEL_SHIFT)) | (n_idx == 0)
        admissible = n_idx * SEL_BLOCK <= t_q
        key = _sort_key(jnp.where(forced, FORCED_SCORE, jnp.where(admissible, imp, -1.0)))
        tau = _bisect(lambda cand: _col_sum(jnp.where(key >= cand, 1, 0)), SEL_TOPN, 32, INT_MIN)
        gt = key > tau
        eq = key == tau
        need = (SEL_TOPN - _col_sum(jnp.where(gt, 1, 0))).astype(F32)
        rank = _nn(tril_ref[...], jnp.where(eq, 1.0, 0.0).astype(BF16))
        selb = jnp.where(gt | (eq & (rank <= need)), 1.0, 0.0).astype(BF16)

        m_scr[...] = jnp.full(m_scr.shape, NEG_INF, F32)
        acc_scr[...] = jnp.zeros_like(acc_scr)

        def sel_chunk(c):
            base = pl.multiple_of(c * KCH, KCH)
            msk = (_nn(ex_ref[c], selb) > 0.5) & (base + k_pos <= q_pos)
            return sk_ref[pl.ds(base, KCH), :], msk, sv_ref[g, c]

        def sel_step(cs, carry):
            _online_softmax_step([sel_chunk(c) for c in cs], q, B_HPG, m_scr, acc_scr)
            return carry

        _for_chunk_pairs(nch, sel_step, 0)
        acc = acc_scr[...]
        o_s = acc[0:HEAD_DIM] / acc[HEAD_DIM:HEAD_DIM + 1]

        wlen = QB + WINDOW
        blk0 = jnp.maximum(i - WINDOW // QB, 0)
        start = pl.multiple_of(blk0 * QB, QB)
        sw = _nt(wk_ref[pl.ds(start, wlen), :], q)
        w_idx = start + lax.broadcasted_iota(I32, (wlen, cols), 0)
        t_w = t_of_col((wlen, cols))
        wmask = (w_idx <= t_w) & (w_idx > t_w - WINDOW)
        swm = jnp.where(wmask, sw, NEG_INF)
        pw = jnp.exp(swm - jnp.max(swm, axis=0, keepdims=True)).astype(BF16)
        accw = jnp.zeros((LANES, cols), F32)
        for j in range(wlen // LANES):
            accw = accw + _nn(wv_ref[g, blk0 + j], pw[j * LANES:(j + 1) * LANES])
        o_w = accw[0:HEAD_DIM] / accw[HEAD_DIM:HEAD_DIM + 1]

        for hh in range(B_HPG):
            h = g * B_HPG + hh
            cs = slice(hh * QB, (hh + 1) * QB)
            r = h * N_BRANCH_GATES
            heads[h] = (gates[r:r + 1] * o_c[:, cs] + gates[r + 1:r + 2] * o_s[:, cs]
                        + gates[r + 2:r + 3] * o_w[:, cs])

    _store_heads(o_ref, heads)


def _nsa(bq, kc, vc, sk, sv, wk, wv, gt, ov, ex, tril, batch, seq):
    nqb = seq // QB
    nkc = seq // KCH
    ncmp = seq // CMP_STRIDE
    sspec = pl.BlockSpec((1, 8, QB, LANES), lambda b, i: (b * nqb + i, 0, 0, 0))
    kspec = pl.BlockSpec((seq, LANES), lambda b, i: (b, 0))
    return pl.pallas_call(
        _nsa_body,
        grid=(batch, nqb),
        in_specs=[sspec,
                  pl.BlockSpec((ncmp, LANES), lambda b, i: (b, 0)),
                  pl.BlockSpec((LANES, ncmp), lambda b, i: (b, 0)),
                  kspec,
                  pl.BlockSpec((B_GROUPS, nkc, LANES, KCH), lambda b, i: (0, b, 0, 0)),
                  kspec,
                  pl.BlockSpec((B_GROUPS, seq // LANES, LANES, LANES), lambda b, i: (0, b, 0, 0)),
                  pl.BlockSpec((GATE_ROWS, QB), lambda b, i: (0, b * nqb + i)),
                  pl.BlockSpec((NSEL, ncmp), lambda b, i: (0, 0)),
                  pl.BlockSpec((nkc, KCH, NSEL), lambda b, i: (0, 0, 0)),
                  pl.BlockSpec((NSEL, NSEL), lambda b, i: (0, 0))],
        out_specs=pl.BlockSpec((QB, B_HEADS * HEAD_DIM), lambda b, i: (b * nqb + i, 0)),
        out_shape=jax.ShapeDtypeStruct((batch * seq, B_HEADS * HEAD_DIM), BF16),
        scratch_shapes=[pltpu.VMEM((1, B_HPG * QB), F32),
                        pltpu.VMEM((LANES, B_HPG * QB), F32)],
        compiler_params=_cparams(("parallel", "arbitrary")),
        name="nsa",
    )(bq, kc, vc, sk, sv, wk, wv, gt, ov, ex, tril)


def _merge_body(x_ref, oa_ref, ob_ref, gm_ref, wpa_ref, wpb_ref, wo_ref, o_ref):
    gm = gm_ref[...]
    ga = _sigmoid(gm[:, :D_MODEL])
    gb = _sigmoid(gm[:, D_MODEL:])
    merged = ga * _nn(oa_ref[...], wpa_ref[...]) + gb * _nn(ob_ref[...], wpb_ref[...])
    o_ref[...] = x_ref[...] + _nn(merged.astype(BF16), wo_ref[...])


def _merge(x, oa, ob, z, wpa, wpb, wo, tm=512):
    m, d = x.shape
    full = lambda a: pl.BlockSpec(a.shape, lambda i: (0, 0))
    return pl.pallas_call(
        _merge_body,
        grid=(m // tm,),
        in_specs=[pl.BlockSpec((tm, d), lambda i: (i, 0)),
                  pl.BlockSpec((tm, oa.shape[1]), lambda i: (i, 0)),
                  pl.BlockSpec((tm, ob.shape[1]), lambda i: (i, 0)),
                  pl.BlockSpec((tm, 2 * d), lambda i: (i, OFF_GM // (2 * d))),
                  full(wpa), full(wpb), full(wo)],
        out_specs=pl.BlockSpec((tm, d), lambda i: (i, 0)),
        out_shape=jax.ShapeDtypeStruct((m, d), F32),
        compiler_params=_cparams(("parallel",)),
        name="merge",
    )(x, oa, ob, z, wpa, wpb, wo)


def _ple_body(x_ref, p_ref, g_ref, wg_ref, wp_ref, fn_ref, o_ref, *, final):
    x = x_ref[...]
    gate = _sigmoid(_nn(_rms(x, g_ref[...]).astype(BF16), wg_ref[...]))
    y = x + gate * _nn(p_ref[...].astype(BF16), wp_ref[...])
    o_ref[...] = _rms(y, fn_ref[...]) if final else y


def _ple(x, p, g, wg, wp, fn, final, tm=512):
    m, d = x.shape
    full = lambda a: pl.BlockSpec(a.shape, lambda i: (0, 0))
    return pl.pallas_call(
        functools.partial(_ple_body, final=final),
        grid=(m // tm,),
        in_specs=[pl.BlockSpec((tm, d), lambda i: (i, 0)),
                  pl.BlockSpec((tm, p.shape[1]), lambda i: (i, 0)),
                  full(g), full(wg), full(wp), full(fn)],
        out_specs=pl.BlockSpec((tm, d), lambda i: (i, 0)),
        out_shape=jax.ShapeDtypeStruct((m, d), F32),
        compiler_params=_cparams(("parallel",)),
        name="ple",
    )(x, p, g, wg, wp, fn)


def _pad_w_in(w_in):
    sizes = (512, 128, 256, 32, 8, 512, 768, 24, 2048)
    offs = (OFF_QA, OFF_CKV, OFF_QI, OFF_KI, OFF_WI, OFF_QB, OFF_KVB, OFF_GB, OFF_GM)
    out = jnp.zeros(w_in.shape[:2] + (Z_COLS,), BF16)
    src = 0
    for size, off in zip(sizes, offs):
        out = out.at[:, :, off:off + size].set(w_in[:, :, src:src + size].astype(BF16))
        src += size
    return out


def _cmp_w1(w1):
    nl = w1.shape[0]
    w = w1.reshape(nl, 2, CMP_STRIDE, HEAD_DIM, CMP_HIDDEN).astype(BF16)
    out = jnp.zeros((nl, B_GROUPS, 2, CMP_STRIDE, B_GROUPS, HEAD_DIM, CMP_HIDDEN), BF16)
    for g in range(B_GROUPS):
        out = out.at[:, g, :, :, g].set(w)
    return out.reshape(nl, B_GROUPS, 2, CMP_STRIDE * LANES, CMP_HIDDEN)


def _cmp_w2(w2):
    nl = w2.shape[0]
    out = jnp.zeros((nl, B_GROUPS, CMP_HIDDEN, B_GROUPS, HEAD_DIM), BF16)
    for g in range(B_GROUPS):
        out = out.at[:, g, :, g].set(w2.astype(BF16))
    return out.reshape(nl, B_GROUPS, CMP_HIDDEN, LANES)


def _cmp_pos(cmp_pos):
    nl = cmp_pos.shape[0]
    p = cmp_pos.reshape(nl, 2, CMP_STRIDE, 1, HEAD_DIM)
    return jnp.broadcast_to(p, (nl, 2, CMP_STRIDE, B_GROUPS, HEAD_DIM)).reshape(nl, 2, CMP_STRIDE * LANES)


def _masks(seq):
    ncmp = seq // CMP_STRIDE
    c = np.arange(ncmp)[:, None] * CMP_STRIDE
    n = np.arange(NSEL)[None, :] * SEL_BLOCK
    n_real = (seq - CMP_BLOCK) // CMP_STRIDE + 1
    overlap = (c < n + SEL_BLOCK) & (n < c + CMP_BLOCK) & (np.arange(ncmp)[:, None] < n_real)
    s = np.arange(seq).reshape(seq // KCH, KCH, 1)
    expand = (s // SEL_BLOCK) == np.arange(NSEL)[None, None, :]
    tril = lambda n: np.arange(n)[None, :] <= np.arange(n)[:, None]
    return (jnp.asarray(overlap.T, BF16), jnp.asarray(expand, BF16),
            jnp.asarray(tril(KCH), BF16), jnp.asarray(tril(NSEL), BF16))


def kernel(x, p, ffa_norm, ffa_w1, ffa_w3, ffa_w2, mix_norm, w_in, a_kv_norm, a_w_ukv, cmp_pos, cmp_k_w1, cmp_k_w2, cmp_v_w1, cmp_v_w2, w_proj_a, w_proj_b, w_out, ffb_norm, ffb_w1, ffb_w3, ffb_w2, ple_norm, ple_w_gate, ple_w_proj, final_norm):
    batch, seq, d = x.shape
    depth = w_in.shape[0]
    m = batch * seq
    bf = lambda a: a.astype(BF16)
    ffa_w1, ffa_w3, ffa_w2 = bf(ffa_w1), bf(ffa_w3), bf(ffa_w2)
    ffb_w1, ffb_w3, ffb_w2 = bf(ffb_w1), bf(ffb_w3), bf(ffb_w2)
    w_in_p = _pad_w_in(w_in)
    ukv = bf(a_w_ukv)
    wk1, wk2 = _cmp_w1(cmp_k_w1), _cmp_w2(cmp_k_w2)
    wv1, wv2 = _cmp_w1(cmp_v_w1), _cmp_w2(cmp_v_w2)
    pos2 = _cmp_pos(cmp_pos)
    wpa, wpb, wo = bf(w_proj_a), bf(w_proj_b), bf(w_out)
    wg, wp = bf(ple_w_gate), bf(ple_w_proj)
    tabs = _rope_tables(seq)
    overlap, expand, tril_k, tril_n = _masks(seq)
    fn = final_norm.reshape(1, d)

    xf = x.reshape(m, d)
    pf = p.reshape(depth, m, p.shape[-1])
    for i in range(depth):
        xf = _ffn(xf, ffa_norm[i].reshape(1, d), ffa_w1[i], ffa_w3[i], ffa_w2[i])
        z = _inproj(xf, mix_norm[i].reshape(1, d), w_in_p[i])
        (aq, ak, av, iq, ik, iw, bq, ck, cv, sk, sv, wk, wv, gt) = _prep(
            z, tabs, a_kv_norm[i].reshape(1, A_KV_RANK), ukv[i], seq)
        kc, vc = _compress(ck, cv, pos2[i], wk1[i], wk2[i], wv1[i], wv2[i], batch, seq)
        oa = _dsa(aq, iq, iw, ak, av, ik, tril_k, batch, seq)
        ob = _nsa(bq, kc, vc, sk, sv, wk, wv, gt, overlap, expand, tril_n, batch, seq)
        xf = _merge(xf, oa, ob, z, wpa[i], wpb[i], wo[i])
        xf = _ffn(xf, ffb_norm[i].reshape(1, d), ffb_w1[i], ffb_w3[i], ffb_w2[i])
        xf = _ple(xf, pf[i], ple_norm[i].reshape(1, d), wg[i], wp[i], fn, final=(i == depth - 1))
    return xf.reshape(batch, seq, d)
```

```python
import functools

import numpy as np
import jax
import jax.numpy as jnp
from jax import lax
from jax.experimental import pallas as pl
from jax.experimental.pallas import tpu as pltpu

F32 = jnp.float32
BF16 = jnp.bfloat16
I32 = jnp.int32

D_MODEL = 1024
DEPTH = 4
PLE_DIM = 256
D_FF = 2816
ROPE_THETA = 500000.0
HEAD_DIM = 64
ROT_DIM = HEAD_DIM // 4
NEG_INF = -1e30
RMS_EPS = 1e-6
A_HEADS = 8
A_KV_RANK = 128
IDX_HEADS = 8
IDX_DIM = 32
IDX_ROT_DIM = IDX_DIM // 4
A_TOPK_MAX = 256
B_HEADS = 8
B_GROUPS = 2
B_HPG = B_HEADS // B_GROUPS
CMP_BLOCK = 32
CMP_STRIDE = 16
CMP_HIDDEN = 256
SEL_BLOCK = 64
SEL_TOPN = 16
WINDOW = 512
FORCED_SCORE = 1e4
N_BRANCH_GATES = 3

LANES = 128
QB = 128
KCH = 512
NSEL = 128
GATE_ROWS = 32
SEL_SHIFT = SEL_BLOCK.bit_length() - 1
assert 1 << SEL_SHIFT == SEL_BLOCK
PACK = 16
CGRP = 4
VROWS = HEAD_DIM + PACK
INT_MIN = -2147483648
KEY_NEG_INF = -2139095041
I16 = jnp.int16

OFF_QA, OFF_QB, OFF_KVB, OFF_GM = 0, 512, 1024, 2048
OFF_QI, OFF_CKV, OFF_KI, OFF_WI, OFF_GB = 4096, 4352, 4480, 4608, 4736
Z_COLS = 4864
VMEM_LIMIT = 56 * 1024 * 1024


def _nt(a, b):
    return lax.dot_general(a, b, (((1,), (1,)), ((), ())), preferred_element_type=F32)


def _nn(a, b):
    return jnp.dot(a, b, preferred_element_type=F32)


def _cparams(sem):
    return pltpu.CompilerParams(dimension_semantics=sem, vmem_limit_bytes=VMEM_LIMIT)


def _rms(x, g):
    ms = jnp.mean(x * x, axis=-1, keepdims=True)
    return x * lax.rsqrt(ms + RMS_EPS) * g


def _sigmoid(x):
    return 1.0 / (1.0 + jnp.exp(-x))


def _ffn_body(x_ref, g_ref, w1_ref, w3_ref, w2_ref, o_ref, h_scr, acc_scr):
    j = pl.program_id(1)

    @pl.when(j == 0)
    def _():
        h_scr[...] = _rms(x_ref[...], g_ref[...]).astype(BF16)
        acc_scr[...] = jnp.zeros_like(acc_scr)

    h = h_scr[...]
    a = _nn(h, w1_ref[...])
    b = _nn(h, w3_ref[...])
    u = (a * _sigmoid(a)) * b
    acc_scr[...] += _nn(u.astype(BF16), w2_ref[...])

    @pl.when(j == pl.num_programs(1) - 1)
    def _():
        o_ref[...] = x_ref[...] + 0.5 * acc_scr[...]


def _ffn(x, g, w1, w3, w2, layer, tm=512, tf=1408):
    m, d = x.shape
    f = w1.shape[2]
    return pl.pallas_call(
        _ffn_body,
        grid=(m // tm, f // tf),
        in_specs=[
            pl.BlockSpec((tm, d), lambda i, j: (i, 0)),
            pl.BlockSpec((None, 1, d), lambda i, j: (layer, 0, 0)),
            pl.BlockSpec((None, d, tf), lambda i, j: (layer, 0, j)),
            pl.BlockSpec((None, d, tf), lambda i, j: (layer, 0, j)),
            pl.BlockSpec((None, tf, d), lambda i, j: (layer, j, 0)),
        ],
        out_specs=pl.BlockSpec((tm, d), lambda i, j: (i, 0)),
        out_shape=jax.ShapeDtypeStruct((m, d), F32),
        scratch_shapes=[pltpu.VMEM((tm, d), BF16), pltpu.VMEM((tm, d), F32)],
        compiler_params=_cparams(("parallel", "arbitrary")),
        name="ffn",
    )(x, g, w1, w3, w2)


def _inproj_body(x_ref, g_ref, w_ref, o_ref, h_scr):
    @pl.when(pl.program_id(1) == 0)
    def _():
        h_scr[...] = _rms(x_ref[...], g_ref[...]).astype(BF16)

    o_ref[...] = _nn(h_scr[...], w_ref[...])


def _inproj(x, g, w, layer, tm=512, tn=2432):
    m, d = x.shape
    n = w.shape[2]
    return pl.pallas_call(
        _inproj_body,
        grid=(m // tm, n // tn),
        in_specs=[
            pl.BlockSpec((tm, d), lambda i, j: (i, 0)),
            pl.BlockSpec((None, 1, d), lambda i, j: (layer, 0, 0)),
            pl.BlockSpec((None, d, tn), lambda i, j: (layer, 0, j)),
        ],
        out_specs=pl.BlockSpec((tm, tn), lambda i, j: (i, j)),
        out_shape=jax.ShapeDtypeStruct((m, n), F32),
        scratch_shapes=[pltpu.VMEM((tm, d), BF16)],
        compiler_params=_cparams(("parallel", "arbitrary")),
        name="inproj",
    )(x, g, w)


def _rope_tables(seq):
    pos = jnp.arange(seq, dtype=F32)
    lane = np.arange(LANES)
    out = []
    for rot_dim, period, first_only in ((ROT_DIM, HEAD_DIM, False),
                                        (ROT_DIM, HEAD_DIM, True),
                                        (IDX_ROT_DIM, IDX_DIM, False)):
        half = rot_dim // 2
        inv = ROPE_THETA ** (-jnp.arange(0, rot_dim, 2, dtype=F32) / rot_dim)
        ang = pos[:, None] * inv[None, :]
        cos, sin = jnp.cos(ang), jnp.sin(ang)
        lp = lane % period
        in1 = lp < half
        in2 = (lp >= half) & (lp < 2 * half)
        if first_only:
            in1 = in1 & (lane < period)
            in2 = in2 & (lane < period)
        fidx = np.where(in1, lp, np.where(in2, lp - half, 0))
        cos_l, sin_l = cos[:, fidx], sin[:, fidx]
        out.append(jnp.where((in1 | in2)[None, :], cos_l, 1.0))
        out.append(jnp.where(in1[None, :], -sin_l, 0.0))
        out.append(jnp.where(in2[None, :], sin_l, 0.0))
    return jnp.stack(out).astype(F32)


def _rope(x, tab_ref, k, half):
    a, b, c = tab_ref[3 * k], tab_ref[3 * k + 1], tab_ref[3 * k + 2]
    return (x * a + pltpu.roll(x, LANES - half, axis=1) * b
            + pltpu.roll(x, half, axis=1) * c)


def _prep_body(qa_ref, qb_ref, kvb_ref, qi_ref, ckv_ref, ki_ref, wi_ref, gb_ref,
               tab_ref, kvn_ref, ukv_ref,
               aq_ref, ak_ref, av_ref, iq_ref, ik_ref, iw_ref,
               bq_ref, ck_ref, cv_ref, sk_ref, sv_ref, wk_ref, wv_ref, gt_ref):
    tr = qa_ref.shape[0]
    nq = tr // QB
    lane = lax.broadcasted_iota(I32, (tr, LANES), 1)
    lo = lane < HEAD_DIM
    scale = HEAD_DIM ** -0.5
    half = ROT_DIM // 2

    for cg in range(4):
        sl = slice(cg * LANES, (cg + 1) * LANES)
        ya = _rope(qa_ref[:, sl], tab_ref, 0, half) * scale
        ya_sw = pltpu.roll(ya, HEAD_DIM, axis=1)
        even = jnp.where(lo, ya, 0.0).astype(BF16)
        odd = jnp.where(lo, ya_sw, 0.0).astype(BF16)
        yb = _rope(qb_ref[:, sl], tab_ref, 0, half) * scale
        yb_sw = pltpu.roll(yb, HEAD_DIM, axis=1)
        if cg < 2:
            b_even = jnp.where(lo, yb, 0.0).astype(BF16)
            b_odd = jnp.where(lo, yb_sw, 0.0).astype(BF16)
        else:
            b_even = jnp.where(lo, 0.0, yb_sw).astype(BF16)
            b_odd = jnp.where(lo, 0.0, yb).astype(BF16)
        for q in range(nq):
            rows = slice(q * QB, (q + 1) * QB)
            aq_ref[q, 2 * cg] = even[rows]
            aq_ref[q, 2 * cg + 1] = odd[rows]
            bq_ref[q, 2 * cg] = b_even[rows]
            bq_ref[q, 2 * cg + 1] = b_odd[rows]

    first = lane < IDX_DIM
    for cg in range(2):
        y = _rope(qi_ref[:, cg * LANES:(cg + 1) * LANES], tab_ref, 2, IDX_ROT_DIM // 2)
        for k in range(4):
            yk = y if k == 0 else pltpu.roll(y, LANES - IDX_DIM * k, axis=1)
            piece = jnp.where(first, yk, 0.0).astype(BF16)
            for q in range(nq):
                iq_ref[q, 4 * cg + k] = piece[q * QB:(q + 1) * QB]

    ik_ref[...] = _rope(ki_ref[...], tab_ref, 2, IDX_ROT_DIM // 2).astype(BF16)
    iw_ref[...] = jnp.transpose(wi_ref[...] * (IDX_HEADS ** -0.5))[0:IDX_HEADS]

    def vt(v):
        return jnp.transpose(jnp.where(lo, v, 1.0))[0:VROWS].astype(BF16)

    c = _rms(ckv_ref[...], kvn_ref[...]).astype(BF16)
    kv = _rope(_nn(c, ukv_ref[...]), tab_ref, 1, half)
    ak_ref[...] = jnp.where(lo, kv, 0.0).astype(BF16)
    av_ref[0] = vt(pltpu.roll(kv, HEAD_DIM, axis=1))

    ck_ref[...] = _rope(kvb_ref[:, 0:128], tab_ref, 0, half)
    cv_ref[...] = kvb_ref[:, 128:256]
    sk_ref[...] = _rope(kvb_ref[:, 256:384], tab_ref, 0, half).astype(BF16)
    vs = kvb_ref[:, 384:512]
    wk_ref[...] = _rope(kvb_ref[:, 512:640], tab_ref, 0, half).astype(BF16)
    vw = kvb_ref[:, 640:768]
    sv_ref[0, 0] = vt(vs)
    sv_ref[1, 0] = vt(pltpu.roll(vs, HEAD_DIM, axis=1))
    for g, v in enumerate((vw, pltpu.roll(vw, HEAD_DIM, axis=1))):
        t = vt(v)
        for j in range(tr // LANES):
            wv_ref[g, j] = t[:, j * LANES:(j + 1) * LANES]
    gt_ref[...] = jnp.transpose(_sigmoid(gb_ref[...]))[0:GATE_ROWS]


def _prep(z, tabs, kvn, ukv, seq):
    tr = KCH
    m = z.shape[0]
    nq = tr // QB
    nqb_total = m // QB
    spt = seq // tr

    def zspec(width, off):
        return pl.BlockSpec((tr, width), lambda i, o=off // width: (i, o))

    row = lambda dt: jax.ShapeDtypeStruct((m, LANES), dt)
    rspec = pl.BlockSpec((tr, LANES), lambda i: (i, 0))
    stack = jax.ShapeDtypeStruct((nqb_total, 8, QB, LANES), BF16)
    sspec = pl.BlockSpec((nq, 8, QB, LANES), lambda i: (i, 0, 0, 0))
    colT = lambda r: (jax.ShapeDtypeStruct((r, m), F32), pl.BlockSpec((r, tr), lambda i: (0, i)))
    iw_shape, iw_spec = colT(IDX_HEADS)
    gt_shape, gt_spec = colT(GATE_ROWS)
    av_shape = jax.ShapeDtypeStruct((m // tr, VROWS, tr), BF16)
    av_spec = pl.BlockSpec((1, VROWS, tr), lambda i: (i, 0, 0))
    sv_shape = jax.ShapeDtypeStruct((B_GROUPS, m // tr, VROWS, tr), BF16)
    sv_spec = pl.BlockSpec((B_GROUPS, 1, VROWS, tr), lambda i: (0, i, 0, 0))
    wv_shape = jax.ShapeDtypeStruct((B_GROUPS, m // LANES, VROWS, LANES), BF16)
    wv_spec = pl.BlockSpec((B_GROUPS, tr // LANES, VROWS, LANES), lambda i: (0, i, 0, 0))
    return pl.pallas_call(
        _prep_body,
        grid=(m // tr,),
        in_specs=[
            zspec(512, OFF_QA), zspec(512, OFF_QB), zspec(1024, OFF_KVB),
            zspec(256, OFF_QI), zspec(128, OFF_CKV), zspec(128, OFF_KI),
            zspec(128, OFF_WI), zspec(128, OFF_GB),
            pl.BlockSpec((9, tr, LANES), lambda i: (0, i % spt, 0)),
            pl.BlockSpec((1, LANES), lambda i: (0, 0)),
            pl.BlockSpec((LANES, LANES), lambda i: (0, 0)),
        ],
        out_specs=[sspec, rspec, av_spec, sspec, rspec, iw_spec,
                   sspec, rspec, rspec, rspec, sv_spec, rspec, wv_spec, gt_spec],
        out_shape=[stack, row(BF16), av_shape, stack, row(BF16), iw_shape,
                   stack, row(F32), row(F32), row(BF16), sv_shape, row(BF16), wv_shape, gt_shape],
        compiler_params=_cparams(("parallel",)),
        name="prep",
    )(z, z, z, z, z, z, z, z, tabs, kvn, ukv)


def _compress_body(xk_ref, xv_ref, pos_ref, wk1_ref, wk2_ref, wv1_ref, wv2_ref, kc_ref, vc_ref):
    nh = xk_ref.shape[0]

    def one(x_ref, w1_ref, w2_ref):
        x = x_ref[...]
        top = (x + pos_ref[0:1, :]).astype(BF16)
        bot = (x + pos_ref[1:2, :]).astype(BF16)
        out = jnp.zeros((nh, LANES), F32)
        for g in range(B_GROUPS):
            a = _nn(top, w1_ref[g, 0])
            b = _nn(bot, w1_ref[g, 1])
            hid = a + pltpu.roll(b, nh - 1, axis=0)
            act = (hid * _sigmoid(hid)).astype(BF16)
            out = out + _nn(act, w2_ref[g])
        return out

    kc_ref[...] = one(xk_ref, wk1_ref, wk2_ref).astype(BF16)
    vc_ref[...] = jnp.transpose(one(xv_ref, wv1_ref, wv2_ref)).astype(BF16)


def _compress(ck, cv, pos2, wk1, wk2, wv1, wv2, batch, seq):
    nh = seq // CMP_STRIDE
    width = CMP_STRIDE * LANES
    xk = ck.reshape(batch * nh, width)
    xv = cv.reshape(batch * nh, width)
    xspec = pl.BlockSpec((nh, width), lambda b: (b, 0))
    w1spec = pl.BlockSpec((B_GROUPS, 2, width, CMP_HIDDEN), lambda b: (0, 0, 0, 0))
    w2spec = pl.BlockSpec((B_GROUPS, CMP_HIDDEN, LANES), lambda b: (0, 0, 0))
    return pl.pallas_call(
        _compress_body,
        grid=(batch,),
        in_specs=[xspec, xspec, pl.BlockSpec((2, width), lambda b: (0, 0)),
                  w1spec, w2spec, w1spec, w2spec],
        out_specs=[pl.BlockSpec((nh, LANES), lambda b: (b, 0)),
                   pl.BlockSpec((LANES, nh), lambda b: (b, 0))],
        out_shape=[jax.ShapeDtypeStruct((batch * nh, LANES), BF16),
                   jax.ShapeDtypeStruct((batch * LANES, nh), BF16)],
        compiler_params=_cparams(("parallel",)),
        name="compress",
    )(xk, xv, pos2, wk1, wk2, wv1, wv2)


def _sort_key(x):
    x = jnp.where(x == 0.0, 0.0, x)
    bits = lax.bitcast_convert_type(x, I32)
    return bits ^ ((bits >> 31) & 0x7FFFFFFF)


def _col_sum(x):
    return jnp.sum(x, axis=0, keepdims=True)


def _bisect(count_ge, k, nbits, lowest):
    def step(it, tau):
        cand = tau + lax.shift_left(jnp.int32(1), nbits - 1 - it)
        return jnp.where(count_ge(cand) >= k, cand, tau)

    return lax.fori_loop(0, nbits, step, jnp.full((1, QB), lowest, I32))


def _pack_row(v):
    return jnp.broadcast_to(v, (PACK, QB)).astype(I16)[None]


def _count16(scr, ngrp, pred):
    def body(gi, cnt):
        for j in range(CGRP):
            v = scr[gi * CGRP + j].reshape(KCH // PACK, PACK, QB)
            hit = jnp.where(pred(v), jnp.int16(1), jnp.int16(0))
            cnt = cnt + _tree_sum([hit[k] for k in range(KCH // PACK)])
        return cnt

    cnt = lax.fori_loop(0, ngrp, body, jnp.zeros((PACK, QB), I16))
    return _col_sum(cnt.astype(I32))


def _tree_sum(xs):
    while len(xs) > 1:
        xs = [xs[k] + xs[k + 1] for k in range(0, len(xs) - 1, 2)] + ([xs[-1]] if len(xs) % 2 else [])
    return xs[0]


def _online_softmax_step(chunks, q, heads, m_scr, acc_scr):
    scores = [_nt(k_chunk, q).astype(BF16) for k_chunk, _, _ in chunks]
    m_prev = m_scr[...]
    acc = acc_scr[...]
    for (_, sel, v_t), s in zip(chunks, scores):
        sel16 = jnp.where(sel, 1.0, 0.0).astype(BF16) > 0.5
        sm = jnp.concatenate(
            [jnp.where(sel16, s[:, h * QB:(h + 1) * QB], NEG_INF) for h in range(heads)], axis=1)
        m_new = jnp.maximum(m_prev, jnp.max(sm, axis=0, keepdims=True).astype(F32))
        p = jnp.exp(sm - m_new.astype(BF16))
        acc = jnp.exp(m_prev - m_new) * acc + _nn(v_t, p)
        m_prev = m_new
    m_scr[...] = m_prev
    acc_scr[...] = acc


def _for_chunk_pairs(nch, step, init):
    carry = lax.fori_loop(0, nch // 2, lambda c2, cr: step([2 * c2, 2 * c2 + 1], cr), init)

    @pl.when(nch % 2 == 1)
    def _():
        step([nch - 1], carry)


def _store_heads(o_ref, heads_t):
    for cg in range(len(heads_t) // 2):
        blk = jnp.concatenate([heads_t[2 * cg], heads_t[2 * cg + 1]], axis=0)
        o_ref[:, cg * LANES:(cg + 1) * LANES] = jnp.transpose(blk).astype(BF16)


def _dsa_body(aq_ref, iq_ref, iw_ref, ak_ref, av_ref, ik_ref, tril_ref, o_ref,
              key_scr, hi_scr, lo_scr, m_scr, acc_scr, *, top_k):
    i = pl.program_id(1)
    q0 = i * QB
    nch = i // (KCH // QB) + 1
    cols = A_HEADS * QB
    k_pos = lax.broadcasted_iota(I32, (KCH, QB), 0)
    q_pos = q0 + lax.broadcasted_iota(I32, (KCH, QB), 1)
    iw = iw_ref[...]
    iq = iq_ref[0].reshape(cols, LANES)
    ngrp = i // (CGRP * KCH // QB) + 1

    @pl.when(i == 0)
    def _():
        key_scr[...] = jnp.full(key_scr.shape, INT_MIN, I32)
        hi_scr[...] = jnp.full(hi_scr.shape, -32768, I16)
        lo_scr[...] = jnp.full(lo_scr.shape, -32768, I16)

    def score_step(cs, carry):
        rs = [_nt(ik_ref[pl.ds(pl.multiple_of(c * KCH, KCH), KCH), :], iq) for c in cs]
        for c, r in zip(cs, rs):
            sc = jnp.zeros((KCH, QB), F32)
            for h in range(IDX_HEADS):
                sc = sc + jnp.maximum(r[:, h * QB:(h + 1) * QB], 0.0) * iw[h:h + 1, :]
            sc = jnp.where(c * KCH + k_pos <= q_pos, sc, -jnp.inf)
            key = _sort_key(sc)
            key_scr[c] = key
            hi_scr[c] = (key >> 16).astype(I16)
        return carry

    _for_chunk_pairs(nch, score_step, 0)

    def kth16(scr, k):
        return _bisect(
            lambda cand: _count16(scr, ngrp, lambda v, c=_pack_row(cand): v >= c), k, 16, -32768)

    t_hi = kth16(hi_scr, top_k)
    t_hi_p = _pack_row(t_hi)
    n_gt_hi = _count16(hi_scr, ngrp, lambda v: v > t_hi_p)

    def fill_lo(gi, carry):
        for j in range(CGRP):
            c = gi * CGRP + j
            lo = ((key_scr[c] & 0xFFFF) - 32768).astype(I16).reshape(KCH // PACK, PACK, QB)
            in_bucket = hi_scr[c].reshape(KCH // PACK, PACK, QB) == t_hi_p
            lo_scr[c] = jnp.where(in_bucket, lo, jnp.int16(-32768)).reshape(KCH, QB)
        return carry

    lax.fori_loop(0, ngrp, fill_lo, 0)
    t_lo = kth16(lo_scr, top_k - n_gt_hi)
    t_lo_p = _pack_row(t_lo)
    tau = t_hi * 65536 + (t_lo + 32768)
    n_gt = n_gt_hi + _count16(lo_scr, ngrp, lambda v: v > t_lo_p)
    n_eq = _count16(lo_scr, ngrp, lambda v: v == t_lo_p)
    need = top_k - n_gt
    surplus = jnp.max(jnp.where((n_eq > need) & (tau > KEY_NEG_INF), 1, 0))
    need_f = need.astype(F32)

    aq = aq_ref[0].reshape(cols, LANES)
    m_scr[...] = jnp.full(m_scr.shape, NEG_INF, F32)
    acc_scr[...] = jnp.zeros_like(acc_scr)

    def chunk(c, sel):
        base = pl.multiple_of(c * KCH, KCH)
        return ak_ref[pl.ds(base, KCH), :], sel & (base + k_pos <= q_pos), av_ref[c]

    @pl.when(surplus == 0)
    def _():
        def step(cs, carry):
            _online_softmax_step([chunk(c, key_scr[c] >= tau) for c in cs],
                                 aq, A_HEADS, m_scr, acc_scr)
            return carry

        _for_chunk_pairs(nch, step, 0)

    @pl.when(surplus != 0)
    def _():
        def step(cs, tie_seen):
            chunks = []
            for c in cs:
                key = key_scr[c]
                eq = key == tau
                eqf = jnp.where(eq, 1.0, 0.0)
                rank = tie_seen + _nn(tril_ref[...], eqf.astype(BF16))
                chunks.append(chunk(c, (key > tau) | (eq & (rank <= need_f))))
                tie_seen = tie_seen + _col_sum(eqf)
            _online_softmax_step(chunks, aq, A_HEADS, m_scr, acc_scr)
            return tie_seen

        _for_chunk_pairs(nch, step, jnp.zeros((1, QB), F32))

    acc = acc_scr[...]
    out = acc[0:HEAD_DIM] / acc[HEAD_DIM:HEAD_DIM + 1]
    _store_heads(o_ref, [out[:, h * QB:(h + 1) * QB] for h in range(A_HEADS)])


def _dsa(aq, iq, iw, ak, av, ik, tril, batch, seq):
    nqb = seq // QB
    nkc = seq // KCH
    top_k = min(A_TOPK_MAX, seq // 4)
    sspec = pl.BlockSpec((1, 8, QB, LANES), lambda b, i: (b * nqb + i, 0, 0, 0))
    kspec = pl.BlockSpec((seq, LANES), lambda b, i: (b, 0))
    return pl.pallas_call(
        functools.partial(_dsa_body, top_k=top_k),
        grid=(batch, nqb),
        in_specs=[sspec, sspec, pl.BlockSpec((IDX_HEADS, QB), lambda b, i: (0, b * nqb + i)),
                  kspec, pl.BlockSpec((nkc, VROWS, KCH), lambda b, i: (b, 0, 0)), kspec,
                  pl.BlockSpec((KCH, KCH), lambda b, i: (0, 0))],
        out_specs=pl.BlockSpec((QB, A_HEADS * HEAD_DIM), lambda b, i: (b * nqb + i, 0)),
        out_shape=jax.ShapeDtypeStruct((batch * seq, A_HEADS * HEAD_DIM), BF16),
        scratch_shapes=[
            pltpu.VMEM((nkc, KCH, QB), I32),
            pltpu.VMEM((nkc, KCH, QB), I16),
            pltpu.VMEM((nkc, KCH, QB), I16),
            pltpu.VMEM((1, A_HEADS * QB), F32),
            pltpu.VMEM((VROWS, A_HEADS * QB), F32),
        ],
        compiler_params=_cparams(("arbitrary", "arbitrary")),
        name="dsa",
    )(aq, iq, iw, ak, av, ik, tril)


def _nsa_body(bq_ref, kc_ref, vc_ref, sk_ref, sv_ref, wk_ref, wv_ref, gt_ref,
              ov_ref, tril_ref, o_ref, m_scr, acc_scr, sel_scr):
    i = pl.program_id(1)
    q0 = i * QB
    nch = i // (KCH // QB) + 1
    cols = B_HPG * QB
    ncmp = kc_ref.shape[0]
    gates = gt_ref[...]
    n_idx = lax.broadcasted_iota(I32, (NSEL, QB), 0)
    t_q = q0 + lax.broadcasted_iota(I32, (NSEL, QB), 1)
    k_pos = lax.broadcasted_iota(I32, (KCH, QB), 0)
    q_pos = q0 + lax.broadcasted_iota(I32, (KCH, QB), 1)

    def t_of_col(shape):
        return q0 + (lax.broadcasted_iota(I32, shape, 1) & (QB - 1))

    heads = [None] * B_HEADS
    for g in range(B_GROUPS):
        q = bq_ref[0, g * B_HPG:(g + 1) * B_HPG].reshape(cols, LANES)
        grp = slice(g * HEAD_DIM, (g + 1) * HEAD_DIM)

        s = _nt(kc_ref[...], q)
        c_end = lax.broadcasted_iota(I32, (ncmp, cols), 0) * CMP_STRIDE + (CMP_BLOCK - 1)
        cmask = c_end <= t_of_col((ncmp, cols))
        sm = jnp.where(cmask, s, NEG_INF)
        e = jnp.exp(sm - jnp.max(sm, axis=0, keepdims=True))
        pb = jnp.where(cmask, e / _col_sum(e), 0.0).astype(BF16)
        o_c = _nn(vc_ref[...], pb)[grp]
        imp_h = _nn(ov_ref[...], pb)
        imp = imp_h[:, 0:QB]
        for hh in range(1, B_HPG):
            imp = imp + imp_h[:, hh * QB:(hh + 1) * QB]

        forced = (n_idx == lax.shift_right_logical(t_q, SEL_SHIFT)) | (n_idx == 0)
        admissible = n_idx * SEL_BLOCK <= t_q
        key = _sort_key(jnp.where(forced, FORCED_SCORE, jnp.where(admissible, imp, -1.0)))
        tau = _bisect(lambda cand: _col_sum(jnp.where(key >= cand, 1, 0)), SEL_TOPN, 32, INT_MIN)
        gt = key > tau
        eq = key == tau
        need = (SEL_TOPN - _col_sum(jnp.where(gt, 1, 0))).astype(F32)
        rank = _nn(tril_ref[...], jnp.where(eq, 1.0, 0.0).astype(BF16))
        sel_scr[...] = jnp.where(gt | (eq & (rank <= need)), 1.0, 0.0)

        m_scr[...] = jnp.full(m_scr.shape, NEG_INF, F32)
        acc_scr[...] = jnp.zeros_like(acc_scr)

        def sel_chunk(c):
            base = pl.multiple_of(c * KCH, KCH)
            blocks = [jnp.broadcast_to(sel_scr[pl.ds(c * (KCH // SEL_BLOCK) + j, 1), :], (SEL_BLOCK, QB))
                      for j in range(KCH // SEL_BLOCK)]
            msk = (jnp.concatenate(blocks, axis=0) > 0.5) & (base + k_pos <= q_pos)
            return sk_ref[pl.ds(base, KCH), :], msk, sv_ref[g, c]

        def sel_step(cs, carry):
            _online_softmax_step([sel_chunk(c) for c in cs], q, B_HPG, m_scr, acc_scr)
            return carry

        _for_chunk_pairs(nch, sel_step, 0)
        acc = acc_scr[...]
        o_s = acc[0:HEAD_DIM] / acc[HEAD_DIM:HEAD_DIM + 1]

        wlen = QB + WINDOW
        blk0 = jnp.maximum(i - WINDOW // QB, 0)
        start = pl.multiple_of(blk0 * QB, QB)
        sw = _nt(wk_ref[pl.ds(start, wlen), :], q)
        w_idx = start + lax.broadcasted_iota(I32, (wlen, cols), 0)
        t_w = t_of_col((wlen, cols))
        wmask = (w_idx <= t_w) & (w_idx > t_w - WINDOW)
        swm = jnp.where(wmask, sw, NEG_INF)
        pw = jnp.exp(swm - jnp.max(swm, axis=0, keepdims=True)).astype(BF16)
        accw = jnp.zeros((VROWS, cols), F32)
        for j in range(wlen // LANES):
            accw = accw + _nn(wv_ref[g, blk0 + j], pw[j * LANES:(j + 1) * LANES])
        o_w = accw[0:HEAD_DIM] / accw[HEAD_DIM:HEAD_DIM + 1]

        for hh in range(B_HPG):
            h = g * B_HPG + hh
            cs = slice(hh * QB, (hh + 1) * QB)
            r = h * N_BRANCH_GATES
            heads[h] = (gates[r:r + 1] * o_c[:, cs] + gates[r + 1:r + 2] * o_s[:, cs]
                        + gates[r + 2:r + 3] * o_w[:, cs])

    _store_heads(o_ref, heads)


def _nsa(bq, kc, vc, sk, sv, wk, wv, gt, ov, tril, batch, seq):
    nqb = seq // QB
    nkc = seq // KCH
    ncmp = seq // CMP_STRIDE
    sspec = pl.BlockSpec((1, 8, QB, LANES), lambda b, i: (b * nqb + i, 0, 0, 0))
    kspec = pl.BlockSpec((seq, LANES), lambda b, i: (b, 0))
    return pl.pallas_call(
        _nsa_body,
        grid=(batch, nqb),
        in_specs=[sspec,
                  pl.BlockSpec((ncmp, LANES), lambda b, i: (b, 0)),
                  pl.BlockSpec((LANES, ncmp), lambda b, i: (b, 0)),
                  kspec,
                  pl.BlockSpec((B_GROUPS, nkc, VROWS, KCH), lambda b, i: (0, b, 0, 0)),
                  kspec,
                  pl.BlockSpec((B_GROUPS, seq // LANES, VROWS, LANES), lambda b, i: (0, b, 0, 0)),
                  pl.BlockSpec((GATE_ROWS, QB), lambda b, i: (0, b * nqb + i)),
                  pl.BlockSpec((NSEL, ncmp), lambda b, i: (0, 0)),
                  pl.BlockSpec((NSEL, NSEL), lambda b, i: (0, 0))],
        out_specs=pl.BlockSpec((QB, B_HEADS * HEAD_DIM), lambda b, i: (b * nqb + i, 0)),
        out_shape=jax.ShapeDtypeStruct((batch * seq, B_HEADS * HEAD_DIM), BF16),
        scratch_shapes=[pltpu.VMEM((1, B_HPG * QB), F32),
                        pltpu.VMEM((VROWS, B_HPG * QB), F32),
                        pltpu.VMEM((NSEL, QB), F32)],
        compiler_params=_cparams(("parallel", "arbitrary")),
        name="nsa",
    )(bq, kc, vc, sk, sv, wk, wv, gt, ov, tril)


def _merge_body(x_ref, oa_ref, ob_ref, gm_ref, wpa_ref, wpb_ref, wo_ref, o_ref):
    gm = gm_ref[...]
    ga = _sigmoid(gm[:, :D_MODEL])
    gb = _sigmoid(gm[:, D_MODEL:])
    merged = ga * _nn(oa_ref[...], wpa_ref[...]) + gb * _nn(ob_ref[...], wpb_ref[...])
    o_ref[...] = x_ref[...] + _nn(merged.astype(BF16), wo_ref[...])


def _merge(x, oa, ob, z, wpa, wpb, wo, layer, tm=512):
    m, d = x.shape
    full = lambda a: pl.BlockSpec((None,) + a.shape[1:], lambda i: (layer, 0, 0))
    return pl.pallas_call(
        _merge_body,
        grid=(m // tm,),
        in_specs=[pl.BlockSpec((tm, d), lambda i: (i, 0)),
                  pl.BlockSpec((tm, oa.shape[1]), lambda i: (i, 0)),
                  pl.BlockSpec((tm, ob.shape[1]), lambda i: (i, 0)),
                  pl.BlockSpec((tm, 2 * d), lambda i: (i, OFF_GM // (2 * d))),
                  full(wpa), full(wpb), full(wo)],
        out_specs=pl.BlockSpec((tm, d), lambda i: (i, 0)),
        out_shape=jax.ShapeDtypeStruct((m, d), F32),
        compiler_params=_cparams(("parallel",)),
        name="merge",
    )(x, oa, ob, z, wpa, wpb, wo)


def _ple_body(x_ref, p_ref, g_ref, wg_ref, wp_ref, fn_ref, o_ref, *, final):
    x = x_ref[...]
    gate = _sigmoid(_nn(_rms(x, g_ref[...]).astype(BF16), wg_ref[...]))
    y = x + gate * _nn(p_ref[...].astype(BF16), wp_ref[...])
    o_ref[...] = _rms(y, fn_ref[...]) if final else y


def _ple(x, p, g, wg, wp, fn, layer, final, tm=512):
    m, d = x.shape
    full = lambda a: pl.BlockSpec((None,) + a.shape[1:], lambda i: (layer, 0, 0))
    return pl.pallas_call(
        functools.partial(_ple_body, final=final),
        grid=(m // tm,),
        in_specs=[pl.BlockSpec((tm, d), lambda i: (i, 0)),
                  pl.BlockSpec((None, tm, p.shape[2]), lambda i: (layer, i, 0)),
                  full(g), full(wg), full(wp), pl.BlockSpec(fn.shape, lambda i: (0, 0))],
        out_specs=pl.BlockSpec((tm, d), lambda i: (i, 0)),
        out_shape=jax.ShapeDtypeStruct((m, d), F32),
        compiler_params=_cparams(("parallel",)),
        name="ple",
    )(x, p, g, wg, wp, fn)


def _pad_w_in(w_in):
    sizes = (512, 128, 256, 32, 8, 512, 768, 24, 2048)
    offs = (OFF_QA, OFF_CKV, OFF_QI, OFF_KI, OFF_WI, OFF_QB, OFF_KVB, OFF_GB, OFF_GM)
    out = jnp.zeros(w_in.shape[:2] + (Z_COLS,), BF16)
    src = 0
    for size, off in zip(sizes, offs):
        out = out.at[:, :, off:off + size].set(w_in[:, :, src:src + size].astype(BF16))
        src += size
    return out


def _cmp_w1(w1):
    nl = w1.shape[0]
    w = w1.reshape(nl, 2, CMP_STRIDE, HEAD_DIM, CMP_HIDDEN).astype(BF16)
    out = jnp.zeros((nl, B_GROUPS, 2, CMP_STRIDE, B_GROUPS, HEAD_DIM, CMP_HIDDEN), BF16)
    for g in range(B_GROUPS):
        out = out.at[:, g, :, :, g].set(w)
    return out.reshape(nl, B_GROUPS, 2, CMP_STRIDE * LANES, CMP_HIDDEN)


def _cmp_w2(w2):
    nl = w2.shape[0]
    out = jnp.zeros((nl, B_GROUPS, CMP_HIDDEN, B_GROUPS, HEAD_DIM), BF16)
    for g in range(B_GROUPS):
        out = out.at[:, g, :, g].set(w2.astype(BF16))
    return out.reshape(nl, B_GROUPS, CMP_HIDDEN, LANES)


def _cmp_pos(cmp_pos):
    nl = cmp_pos.shape[0]
    p = cmp_pos.reshape(nl, 2, CMP_STRIDE, 1, HEAD_DIM)
    return jnp.broadcast_to(p, (nl, 2, CMP_STRIDE, B_GROUPS, HEAD_DIM)).reshape(nl, 2, CMP_STRIDE * LANES)


def _masks(seq):
    ncmp = seq // CMP_STRIDE
    c = np.arange(ncmp)[:, None] * CMP_STRIDE
    n = np.arange(NSEL)[None, :] * SEL_BLOCK
    n_real = (seq - CMP_BLOCK) // CMP_STRIDE + 1
    overlap = (c < n + SEL_BLOCK) & (n < c + CMP_BLOCK) & (np.arange(ncmp)[:, None] < n_real)
    tril = lambda n: np.arange(n)[None, :] <= np.arange(n)[:, None]
    return (jnp.asarray(overlap.T, BF16), jnp.asarray(tril(KCH), BF16), jnp.asarray(tril(NSEL), BF16))


def kernel(x, p, ffa_norm, ffa_w1, ffa_w3, ffa_w2, mix_norm, w_in, a_kv_norm, a_w_ukv, cmp_pos, cmp_k_w1, cmp_k_w2, cmp_v_w1, cmp_v_w2, w_proj_a, w_proj_b, w_out, ffb_norm, ffb_w1, ffb_w3, ffb_w2, ple_norm, ple_w_gate, ple_w_proj, final_norm):
    batch, seq, d = x.shape
    depth = w_in.shape[0]
    m = batch * seq
    bf = lambda a: a.astype(BF16)
    ffa_w1, ffa_w3, ffa_w2 = bf(ffa_w1), bf(ffa_w3), bf(ffa_w2)
    ffb_w1, ffb_w3, ffb_w2 = bf(ffb_w1), bf(ffb_w3), bf(ffb_w2)
    w_in_p = _pad_w_in(w_in)
    ukv = bf(a_w_ukv)
    wk1, wk2 = _cmp_w1(cmp_k_w1), _cmp_w2(cmp_k_w2)
    wv1, wv2 = _cmp_w1(cmp_v_w1), _cmp_w2(cmp_v_w2)
    pos2 = _cmp_pos(cmp_pos)
    wpa, wpb, wo = bf(w_proj_a), bf(w_proj_b), bf(w_out)
    wg, wp = bf(ple_w_gate), bf(ple_w_proj)
    tabs = _rope_tables(seq)
    overlap, tril_k, tril_n = _masks(seq)
    fn = final_norm.reshape(1, d)

    xf = x.reshape(m, d)
    pf = p.reshape(depth, m, p.shape[-1])
    row3 = lambda a: a.reshape(depth, 1, d)
    ffa_norm, ffb_norm, mix_norm, ple_norm = row3(ffa_norm), row3(ffb_norm), row3(mix_norm), row3(ple_norm)
    for i in range(depth):
        xf = _ffn(xf, ffa_norm, ffa_w1, ffa_w3, ffa_w2, i)
        z = _inproj(xf, mix_norm, w_in_p, i)
        (aq, ak, av, iq, ik, iw, bq, ck, cv, sk, sv, wk, wv, gt) = _prep(
            z, tabs, a_kv_norm[i].reshape(1, A_KV_RANK), ukv[i], seq)
        kc, vc = _compress(ck, cv, pos2[i], wk1[i], wk2[i], wv1[i], wv2[i], batch, seq)
        oa = _dsa(aq, iq, iw, ak, av, ik, tril_k, batch, seq)
        ob = _nsa(bq, kc, vc, sk, sv, wk, wv, gt, overlap, tril_n, batch, seq)
        xf = _merge(xf, oa, ob, z, wpa, wpb, wo, i)
        xf = _ffn(xf, ffb_norm, ffb_w1, ffb_w3, ffb_w2, i)
        xf = _ple(xf, pf, ple_norm, wg, wp, fn, i, final=(i == depth - 1))
    return xf.reshape(batch, seq, d)
```

```python
import functools

import numpy as np
import jax
import jax.numpy as jnp
from jax import lax
from jax.experimental import pallas as pl
from jax.experimental.pallas import tpu as pltpu

F32 = jnp.float32
BF16 = jnp.bfloat16
I32 = jnp.int32

D_MODEL = 1024
DEPTH = 4
PLE_DIM = 256
D_FF = 2816
ROPE_THETA = 500000.0
HEAD_DIM = 64
ROT_DIM = HEAD_DIM // 4
NEG_INF = -1e30
RMS_EPS = 1e-6
A_HEADS = 8
A_KV_RANK = 128
IDX_HEADS = 8
IDX_DIM = 32
IDX_ROT_DIM = IDX_DIM // 4
A_TOPK_MAX = 256
B_HEADS = 8
B_GROUPS = 2
B_HPG = B_HEADS // B_GROUPS
CMP_BLOCK = 32
CMP_STRIDE = 16
CMP_HIDDEN = 256
SEL_BLOCK = 64
SEL_TOPN = 16
WINDOW = 512
FORCED_SCORE = 1e4
N_BRANCH_GATES = 3

LANES = 128
QB = 128
KCH = 512
NSEL = 128
GATE_ROWS = 32
SEL_SHIFT = SEL_BLOCK.bit_length() - 1
assert 1 << SEL_SHIFT == SEL_BLOCK
PACK = 16
CGRP = 4
GPC = KCH // (32 * 8)
GPT = CGRP * GPC
VROWS = HEAD_DIM + PACK
INT_MIN = -2147483648
KEY_NEG_INF = -2139095041
NO_LIMIT = 1e9

OFF_QA, OFF_QB, OFF_KVB, OFF_GM = 0, 512, 1024, 2048
OFF_QI, OFF_CKV, OFF_KI, OFF_WI, OFF_GB = 4096, 4352, 4480, 4608, 4736
Z_COLS = 4864
VMEM_LIMIT = 56 * 1024 * 1024


def _nt(a, b):
    return lax.dot_general(a, b, (((1,), (1,)), ((), ())), preferred_element_type=F32)


def _nn(a, b):
    return jnp.dot(a, b, preferred_element_type=F32)


def _cparams(sem):
    return pltpu.CompilerParams(dimension_semantics=sem, vmem_limit_bytes=VMEM_LIMIT)


def _rms(x, g):
    ms = jnp.mean(x * x, axis=-1, keepdims=True)
    return x * lax.rsqrt(ms + RMS_EPS) * g


def _sigmoid(x):
    return 1.0 / (1.0 + jnp.exp(-x))


def _ffn_body(x_ref, g_ref, w1_ref, w3_ref, w2_ref, o_ref, h_scr, acc_scr):
    j = pl.program_id(1)

    @pl.when(j == 0)
    def _():
        h_scr[...] = _rms(x_ref[...], g_ref[...]).astype(BF16)
        acc_scr[...] = jnp.zeros_like(acc_scr)

    h = h_scr[...]
    a = _nn(h, w1_ref[...])
    b = _nn(h, w3_ref[...])
    u = (a * _sigmoid(a)) * b
    acc_scr[...] += _nn(u.astype(BF16), w2_ref[...])

    @pl.when(j == pl.num_programs(1) - 1)
    def _():
        o_ref[...] = x_ref[...] + 0.5 * acc_scr[...]


def _ffn(x, g, w1, w3, w2, layer, tm=512, tf=1408):
    m, d = x.shape
    f = w1.shape[2]
    return pl.pallas_call(
        _ffn_body,
        grid=(m // tm, f // tf),
        in_specs=[
            pl.BlockSpec((tm, d), lambda i, j: (i, 0)),
            pl.BlockSpec((None, 1, d), lambda i, j: (layer, 0, 0)),
            pl.BlockSpec((None, d, tf), lambda i, j: (layer, 0, j)),
            pl.BlockSpec((None, d, tf), lambda i, j: (layer, 0, j)),
            pl.BlockSpec((None, tf, d), lambda i, j: (layer, j, 0)),
        ],
        out_specs=pl.BlockSpec((tm, d), lambda i, j: (i, 0)),
        out_shape=jax.ShapeDtypeStruct((m, d), F32),
        scratch_shapes=[pltpu.VMEM((tm, d), BF16), pltpu.VMEM((tm, d), F32)],
        compiler_params=_cparams(("parallel", "arbitrary")),
        name="ffn",
    )(x, g, w1, w3, w2)


def _inproj_body(x_ref, g_ref, w_ref, o_ref, h_scr):
    @pl.when(pl.program_id(1) == 0)
    def _():
        h_scr[...] = _rms(x_ref[...], g_ref[...]).astype(BF16)

    o_ref[...] = _nn(h_scr[...], w_ref[...])


def _inproj(x, g, w, layer, tm=512, tn=2432):
    m, d = x.shape
    n = w.shape[2]
    return pl.pallas_call(
        _inproj_body,
        grid=(m // tm, n // tn),
        in_specs=[
            pl.BlockSpec((tm, d), lambda i, j: (i, 0)),
            pl.BlockSpec((None, 1, d), lambda i, j: (layer, 0, 0)),
            pl.BlockSpec((None, d, tn), lambda i, j: (layer, 0, j)),
        ],
        out_specs=pl.BlockSpec((tm, tn), lambda i, j: (i, j)),
        out_shape=jax.ShapeDtypeStruct((m, n), F32),
        scratch_shapes=[pltpu.VMEM((tm, d), BF16)],
        compiler_params=_cparams(("parallel", "arbitrary")),
        name="inproj",
    )(x, g, w)


def _rope_tables(seq):
    pos = jnp.arange(seq, dtype=F32)
    lane = np.arange(LANES)
    out = []
    for rot_dim, period, first_only in ((ROT_DIM, HEAD_DIM, False),
                                        (ROT_DIM, HEAD_DIM, True),
                                        (IDX_ROT_DIM, IDX_DIM, False)):
        half = rot_dim // 2
        inv = ROPE_THETA ** (-jnp.arange(0, rot_dim, 2, dtype=F32) / rot_dim)
        ang = pos[:, None] * inv[None, :]
        cos, sin = jnp.cos(ang), jnp.sin(ang)
        lp = lane % period
        in1 = lp < half
        in2 = (lp >= half) & (lp < 2 * half)
        if first_only:
            in1 = in1 & (lane < period)
            in2 = in2 & (lane < period)
        fidx = np.where(in1, lp, np.where(in2, lp - half, 0))
        cos_l, sin_l = cos[:, fidx], sin[:, fidx]
        out.append(jnp.where((in1 | in2)[None, :], cos_l, 1.0))
        out.append(jnp.where(in1[None, :], -sin_l, 0.0))
        out.append(jnp.where(in2[None, :], sin_l, 0.0))
    return jnp.stack(out).astype(F32)


def _rope(x, tab_ref, k, half):
    a, b, c = tab_ref[3 * k], tab_ref[3 * k + 1], tab_ref[3 * k + 2]
    return (x * a + pltpu.roll(x, LANES - half, axis=1) * b
            + pltpu.roll(x, half, axis=1) * c)


def _prep_body(qa_ref, qb_ref, kvb_ref, qi_ref, ckv_ref, ki_ref, wi_ref, gb_ref,
               tab_ref, kvn_ref, ukv_ref,
               aq_ref, ak_ref, av_ref, iq_ref, ik_ref, iw_ref,
               bq_ref, ck_ref, cv_ref, sk_ref, sv_ref, wk_ref, wv_ref, gt_ref):
    tr = qa_ref.shape[0]
    nq = tr // QB
    lane = lax.broadcasted_iota(I32, (tr, LANES), 1)
    lo = lane < HEAD_DIM
    scale = HEAD_DIM ** -0.5
    half = ROT_DIM // 2

    for cg in range(4):
        sl = slice(cg * LANES, (cg + 1) * LANES)
        ya = _rope(qa_ref[:, sl], tab_ref, 0, half) * scale
        ya_sw = pltpu.roll(ya, HEAD_DIM, axis=1)
        even = jnp.where(lo, ya, 0.0).astype(BF16)
        odd = jnp.where(lo, ya_sw, 0.0).astype(BF16)
        yb = _rope(qb_ref[:, sl], tab_ref, 0, half) * scale
        yb_sw = pltpu.roll(yb, HEAD_DIM, axis=1)
        if cg < 2:
            b_even = jnp.where(lo, yb, 0.0).astype(BF16)
            b_odd = jnp.where(lo, yb_sw, 0.0).astype(BF16)
        else:
            b_even = jnp.where(lo, 0.0, yb_sw).astype(BF16)
            b_odd = jnp.where(lo, 0.0, yb).astype(BF16)
        for q in range(nq):
            rows = slice(q * QB, (q + 1) * QB)
            aq_ref[q, 2 * cg] = even[rows]
            aq_ref[q, 2 * cg + 1] = odd[rows]
            bq_ref[q, 2 * cg] = b_even[rows]
            bq_ref[q, 2 * cg + 1] = b_odd[rows]

    first = lane < IDX_DIM
    for cg in range(2):
        y = _rope(qi_ref[:, cg * LANES:(cg + 1) * LANES], tab_ref, 2, IDX_ROT_DIM // 2)
        for k in range(4):
            yk = y if k == 0 else pltpu.roll(y, LANES - IDX_DIM * k, axis=1)
            piece = jnp.where(first, yk, 0.0).astype(BF16)
            for q in range(nq):
                iq_ref[q, 4 * cg + k] = piece[q * QB:(q + 1) * QB]

    ik_ref[...] = _rope(ki_ref[...], tab_ref, 2, IDX_ROT_DIM // 2).astype(BF16)
    iw_ref[...] = jnp.transpose(wi_ref[...] * (IDX_HEADS ** -0.5))[0:IDX_HEADS]

    def vt(v):
        return jnp.transpose(jnp.where(lo, v, 1.0))[0:VROWS].astype(BF16)

    c = _rms(ckv_ref[...], kvn_ref[...]).astype(BF16)
    kv = _rope(_nn(c, ukv_ref[...]), tab_ref, 1, half)
    ak_ref[...] = jnp.where(lo, kv, 0.0).astype(BF16)
    av_ref[0] = vt(pltpu.roll(kv, HEAD_DIM, axis=1))

    ck_ref[...] = _rope(kvb_ref[:, 0:128], tab_ref, 0, half)
    cv_ref[...] = kvb_ref[:, 128:256]
    sk_ref[...] = _rope(kvb_ref[:, 256:384], tab_ref, 0, half).astype(BF16)
    vs = kvb_ref[:, 384:512]
    wk_ref[...] = _rope(kvb_ref[:, 512:640], tab_ref, 0, half).astype(BF16)
    vw = kvb_ref[:, 640:768]
    sv_ref[0, 0] = vt(vs)
    sv_ref[1, 0] = vt(pltpu.roll(vs, HEAD_DIM, axis=1))
    for g, v in enumerate((vw, pltpu.roll(vw, HEAD_DIM, axis=1))):
        t = vt(v)
        for j in range(tr // LANES):
            wv_ref[g, j] = t[:, j * LANES:(j + 1) * LANES]
    gt_ref[...] = jnp.transpose(_sigmoid(gb_ref[...]))[0:GATE_ROWS]


def _prep(z, tabs, kvn, ukv, seq):
    tr = KCH
    m = z.shape[0]
    nq = tr // QB
    nqb_total = m // QB
    spt = seq // tr

    def zspec(width, off):
        return pl.BlockSpec((tr, width), lambda i, o=off // width: (i, o))

    row = lambda dt: jax.ShapeDtypeStruct((m, LANES), dt)
    rspec = pl.BlockSpec((tr, LANES), lambda i: (i, 0))
    stack = jax.ShapeDtypeStruct((nqb_total, 8, QB, LANES), BF16)
    sspec = pl.BlockSpec((nq, 8, QB, LANES), lambda i: (i, 0, 0, 0))
    colT = lambda r: (jax.ShapeDtypeStruct((r, m), F32), pl.BlockSpec((r, tr), lambda i: (0, i)))
    iw_shape, iw_spec = colT(IDX_HEADS)
    gt_shape, gt_spec = colT(GATE_ROWS)
    av_shape = jax.ShapeDtypeStruct((m // tr, VROWS, tr), BF16)
    av_spec = pl.BlockSpec((1, VROWS, tr), lambda i: (i, 0, 0))
    sv_shape = jax.ShapeDtypeStruct((B_GROUPS, m // tr, VROWS, tr), BF16)
    sv_spec = pl.BlockSpec((B_GROUPS, 1, VROWS, tr), lambda i: (0, i, 0, 0))
    wv_shape = jax.ShapeDtypeStruct((B_GROUPS, m // LANES, VROWS, LANES), BF16)
    wv_spec = pl.BlockSpec((B_GROUPS, tr // LANES, VROWS, LANES), lambda i: (0, i, 0, 0))
    return pl.pallas_call(
        _prep_body,
        grid=(m // tr,),
        in_specs=[
            zspec(512, OFF_QA), zspec(512, OFF_QB), zspec(1024, OFF_KVB),
            zspec(256, OFF_QI), zspec(128, OFF_CKV), zspec(128, OFF_KI),
            zspec(128, OFF_WI), zspec(128, OFF_GB),
            pl.BlockSpec((9, tr, LANES), lambda i: (0, i % spt, 0)),
            pl.BlockSpec((1, LANES), lambda i: (0, 0)),
            pl.BlockSpec((LANES, LANES), lambda i: (0, 0)),
        ],
        out_specs=[sspec, rspec, av_spec, sspec, rspec, iw_spec,
                   sspec, rspec, rspec, rspec, sv_spec, rspec, wv_spec, gt_spec],
        out_shape=[stack, row(BF16), av_shape, stack, row(BF16), iw_shape,
                   stack, row(F32), row(F32), row(BF16), sv_shape, row(BF16), wv_shape, gt_shape],
        compiler_params=_cparams(("parallel",)),
        name="prep",
    )(z, z, z, z, z, z, z, z, tabs, kvn, ukv)


def _compress_body(xk_ref, xv_ref, pos_ref, wk1_ref, wk2_ref, wv1_ref, wv2_ref, kc_ref, vc_ref):
    nh = xk_ref.shape[0]

    def one(x_ref, w1_ref, w2_ref):
        x = x_ref[...]
        top = (x + pos_ref[0:1, :]).astype(BF16)
        bot = (x + pos_ref[1:2, :]).astype(BF16)
        out = jnp.zeros((nh, LANES), F32)
        for g in range(B_GROUPS):
            a = _nn(top, w1_ref[g, 0])
            b = _nn(bot, w1_ref[g, 1])
            hid = a + pltpu.roll(b, nh - 1, axis=0)
            act = (hid * _sigmoid(hid)).astype(BF16)
            out = out + _nn(act, w2_ref[g])
        return out

    kc_ref[...] = one(xk_ref, wk1_ref, wk2_ref).astype(BF16)
    vc_ref[...] = jnp.transpose(one(xv_ref, wv1_ref, wv2_ref)).astype(BF16)


def _compress(ck, cv, pos2, wk1, wk2, wv1, wv2, batch, seq):
    nh = seq // CMP_STRIDE
    width = CMP_STRIDE * LANES
    xk = ck.reshape(batch * nh, width)
    xv = cv.reshape(batch * nh, width)
    xspec = pl.BlockSpec((nh, width), lambda b: (b, 0))
    w1spec = pl.BlockSpec((B_GROUPS, 2, width, CMP_HIDDEN), lambda b: (0, 0, 0, 0))
    w2spec = pl.BlockSpec((B_GROUPS, CMP_HIDDEN, LANES), lambda b: (0, 0, 0))
    return pl.pallas_call(
        _compress_body,
        grid=(batch,),
        in_specs=[xspec, xspec, pl.BlockSpec((2, width), lambda b: (0, 0)),
                  w1spec, w2spec, w1spec, w2spec],
        out_specs=[pl.BlockSpec((nh, LANES), lambda b: (b, 0)),
                   pl.BlockSpec((LANES, nh), lambda b: (b, 0))],
        out_shape=[jax.ShapeDtypeStruct((batch * nh, LANES), BF16),
                   jax.ShapeDtypeStruct((batch * LANES, nh), BF16)],
        compiler_params=_cparams(("parallel",)),
        name="compress",
    )(xk, xv, pos2, wk1, wk2, wv1, wv2)


def _sort_key(x):
    x = jnp.where(x == 0.0, 0.0, x)
    bits = lax.bitcast_convert_type(x, I32)
    return bits ^ ((bits >> 31) & 0x7FFFFFFF)


def _col_sum(x):
    return jnp.sum(x, axis=0, keepdims=True)


def _tie_info(tau, n_gt, c_ge, k):
    return tau, jnp.where(c_ge > k, n_gt, -1)


def _kth_largest_small(key, k):
    def step(it, tau):
        cand = tau + lax.shift_left(jnp.int32(1), 31 - it)
        return jnp.where(_col_sum(jnp.where(key >= cand, 1, 0)) >= k, cand, tau)

    tau = lax.fori_loop(0, 32, step, jnp.full((1, QB), INT_MIN, I32))
    return _tie_info(tau, _col_sum(jnp.where(key > tau, 1, 0)), _col_sum(jnp.where(key >= tau, 1, 0)), k)


def _bit_planes(rows):
    a = list(rows)
    j, m = 16, 0x0000FFFF
    while j:
        mask = jnp.int32(m - (1 << 32) if m >= (1 << 31) else m)
        k = 0
        while k < 32:
            t = (a[k] ^ lax.shift_right_logical(a[k + j], jnp.int32(j))) & mask
            a[k] = a[k] ^ t
            a[k + j] = a[k + j] ^ lax.shift_left(t, jnp.int32(j))
            k = (k + j + 1) & ~j
        j >>= 1
        m = (m ^ (m << j)) & 0xFFFFFFFF
    return a


def _kth_largest_planes(plane_scr, e_scr, ngrp, n_real_groups, n_real, k):
    def init(gi, carry):
        for j in range(GPT):
            g = gi * GPT + j
            e_scr[g] = jnp.broadcast_to(jnp.where(g < n_real_groups, -1, 0), (8, QB))
        return carry

    lax.fori_loop(0, ngrp, init, 0)

    def step(it, state):
        tau, n_gt, c_ge, took = state

        def body(gi, cnt):
            for j in range(GPT):
                g = gi * GPT + j
                e = e_scr[g]
                prev = e & plane_scr[g, jnp.maximum(it - 1, 0)]
                e = jnp.where(it > 0, jnp.where(took != 0, prev, e ^ prev), e)
                e_scr[g] = e
                cnt = cnt + lax.population_count(e & plane_scr[g, it])
            return cnt

        tot = n_gt + _col_sum(lax.fori_loop(0, ngrp, body, jnp.zeros((8, QB), I32)))
        take = tot >= k
        tau = jnp.where(take, tau | lax.shift_left(jnp.int32(1), 31 - it), tau)
        return (tau, jnp.where(take, n_gt, tot), jnp.where(take, tot, c_ge),
                jnp.broadcast_to(jnp.where(take, 1, 0), (8, QB)))

    row = lambda v: jnp.full((1, QB), v, I32)
    tau, n_gt, c_ge, _ = lax.fori_loop(
        0, 32, step, (row(0), row(0), row(0) + n_real, jnp.zeros((8, QB), I32)))
    return _tie_info(tau ^ INT_MIN, n_gt, c_ge, k)


def _online_softmax_step(chunks, q, heads, m_scr, acc_scr):
    scores = [_nt(k_chunk, q).astype(BF16) for k_chunk, _, _ in chunks]
    m_prev = m_scr[...]
    acc = acc_scr[...]
    for (_, sel, v_t), s in zip(chunks, scores):
        sel16 = jnp.where(sel, 1.0, 0.0).astype(BF16) > 0.5
        sm = jnp.concatenate(
            [jnp.where(sel16, s[:, h * QB:(h + 1) * QB], NEG_INF) for h in range(heads)], axis=1)
        m_new = jnp.maximum(m_prev, jnp.max(sm, axis=0, keepdims=True).astype(F32))
        p = jnp.exp(sm - m_new.astype(BF16))
        acc = jnp.exp(m_prev - m_new) * acc + _nn(v_t, p)
        m_prev = m_new
    m_scr[...] = m_prev
    acc_scr[...] = acc


def _for_chunk_pairs(nch, step, init):
    carry = lax.fori_loop(0, nch // 2, lambda c2, cr: step([2 * c2, 2 * c2 + 1], cr), init)

    @pl.when(nch % 2 == 1)
    def _():
        step([nch - 1], carry)


def _store_heads(o_ref, heads_t):
    for cg in range(len(heads_t) // 2):
        blk = jnp.concatenate([heads_t[2 * cg], heads_t[2 * cg + 1]], axis=0)
        o_ref[:, cg * LANES:(cg + 1) * LANES] = jnp.transpose(blk).astype(BF16)


def _dsa_body(aq_ref, iq_ref, iw_ref, ak_ref, av_ref, ik_ref, tril_ref, o_ref,
              key_scr, plane_scr, e_scr, m_scr, acc_scr, *, top_k):
    i = pl.program_id(1)
    q0 = i * QB
    nch = i // (KCH // QB) + 1
    cols = A_HEADS * QB
    k_pos = lax.broadcasted_iota(I32, (KCH, QB), 0)
    q_pos = q0 + lax.broadcasted_iota(I32, (KCH, QB), 1)
    iw = iw_ref[...]
    iq = iq_ref[0].reshape(cols, LANES)
    ngrp = i // (CGRP * KCH // QB) + 1

    @pl.when(i == 0)
    def _():
        plane_scr[...] = jnp.zeros(plane_scr.shape, I32)

    def score_step(cs, carry):
        rs = [_nt(ik_ref[pl.ds(pl.multiple_of(c * KCH, KCH), KCH), :], iq) for c in cs]
        for c, r in zip(cs, rs):
            sc = jnp.zeros((KCH, QB), F32)
            for h in range(IDX_HEADS):
                sc = sc + jnp.maximum(r[:, h * QB:(h + 1) * QB], 0.0) * iw[h:h + 1, :]
            sc = jnp.where(c * KCH + k_pos <= q_pos, sc, -jnp.inf)
            key = _sort_key(sc)
            key_scr[c] = key
            rows = (key ^ INT_MIN).reshape(KCH // 8, 8, QB)
            for g in range(GPC):
                planes = _bit_planes([rows[g * 32 + r32] for r32 in range(32)])
                for b in range(32):
                    plane_scr[c * GPC + g, b] = planes[b]
        return carry

    _for_chunk_pairs(nch, score_step, 0)

    tau, n_gt_tie = _kth_largest_planes(plane_scr, e_scr, ngrp, nch * GPC, nch * KCH, top_k)
    surplus = jnp.max(jnp.where((n_gt_tie >= 0) & (tau > KEY_NEG_INF), 1, 0))
    need_f = jnp.where(n_gt_tie >= 0, (top_k - n_gt_tie).astype(F32), NO_LIMIT)

    aq = aq_ref[0].reshape(cols, LANES)
    m_scr[...] = jnp.full(m_scr.shape, NEG_INF, F32)
    acc_scr[...] = jnp.zeros_like(acc_scr)

    def chunk(c, sel):
        base = pl.multiple_of(c * KCH, KCH)
        return ak_ref[pl.ds(base, KCH), :], sel & (base + k_pos <= q_pos), av_ref[c]

    @pl.when(surplus == 0)
    def _():
        def step(cs, carry):
            _online_softmax_step([chunk(c, key_scr[c] >= tau) for c in cs],
                                 aq, A_HEADS, m_scr, acc_scr)
            return carry

        _for_chunk_pairs(nch, step, 0)

    @pl.when(surplus != 0)
    def _():
        def step(cs, tie_seen):
            chunks = []
            for c in cs:
                key = key_scr[c]
                eq = key == tau
                eqf = jnp.where(eq, 1.0, 0.0)
                rank = tie_seen + _nn(tril_ref[...], eqf.astype(BF16))
                chunks.append(chunk(c, (key > tau) | (eq & (rank <= need_f))))
                tie_seen = tie_seen + _col_sum(eqf)
            _online_softmax_step(chunks, aq, A_HEADS, m_scr, acc_scr)
            return tie_seen

        _for_chunk_pairs(nch, step, jnp.zeros((1, QB), F32))

    acc = acc_scr[...]
    out = acc[0:HEAD_DIM] / acc[HEAD_DIM:HEAD_DIM + 1]
    _store_heads(o_ref, [out[:, h * QB:(h + 1) * QB] for h in range(A_HEADS)])


def _dsa(aq, iq, iw, ak, av, ik, tril, batch, seq):
    nqb = seq // QB
    nkc = seq // KCH
    top_k = min(A_TOPK_MAX, seq // 4)
    sspec = pl.BlockSpec((1, 8, QB, LANES), lambda b, i: (b * nqb + i, 0, 0, 0))
    kspec = pl.BlockSpec((seq, LANES), lambda b, i: (b, 0))
    return pl.pallas_call(
        functools.partial(_dsa_body, top_k=top_k),
        grid=(batch, nqb),
        in_specs=[sspec, sspec, pl.BlockSpec((IDX_HEADS, QB), lambda b, i: (0, b * nqb + i)),
                  kspec, pl.BlockSpec((nkc, VROWS, KCH), lambda b, i: (b, 0, 0)), kspec,
                  pl.BlockSpec((KCH, KCH), lambda b, i: (0, 0))],
        out_specs=pl.BlockSpec((QB, A_HEADS * HEAD_DIM), lambda b, i: (b * nqb + i, 0)),
        out_shape=jax.ShapeDtypeStruct((batch * seq, A_HEADS * HEAD_DIM), BF16),
        scratch_shapes=[
            pltpu.VMEM((nkc, KCH, QB), I32),
            pltpu.VMEM((nkc * GPC, 32, 8, QB), I32),
            pltpu.VMEM((nkc * GPC, 8, QB), I32),
            pltpu.VMEM((1, A_HEADS * QB), F32),
            pltpu.VMEM((VROWS, A_HEADS * QB), F32),
        ],
        compiler_params=_cparams(("arbitrary", "arbitrary")),
        name="dsa",
    )(aq, iq, iw, ak, av, ik, tril)


def _nsa_body(bq_ref, kc_ref, vc_ref, sk_ref, sv_ref, wk_ref, wv_ref, gt_ref,
              ov_ref, tril_ref, o_ref, m_scr, acc_scr, sel_scr):
    i = pl.program_id(1)
    q0 = i * QB
    nch = i // (KCH // QB) + 1
    cols = B_HPG * QB
    ncmp = kc_ref.shape[0]
    gates = gt_ref[...]
    n_idx = lax.broadcasted_iota(I32, (NSEL, QB), 0)
    t_q = q0 + lax.broadcasted_iota(I32, (NSEL, QB), 1)
    k_pos = lax.broadcasted_iota(I32, (KCH, QB), 0)
    q_pos = q0 + lax.broadcasted_iota(I32, (KCH, QB), 1)

    def t_of_col(shape):
        return q0 + (lax.broadcasted_iota(I32, shape, 1) & (QB - 1))

    heads = [None] * B_HEADS
    for g in range(B_GROUPS):
        q = bq_ref[0, g * B_HPG:(g + 1) * B_HPG].reshape(cols, LANES)
        grp = slice(g * HEAD_DIM, (g + 1) * HEAD_DIM)

        s = _nt(kc_ref[...], q)
        c_end = lax.broadcasted_iota(I32, (ncmp, cols), 0) * CMP_STRIDE + (CMP_BLOCK - 1)
        cmask = c_end <= t_of_col((ncmp, cols))
        sm = jnp.where(cmask, s, NEG_INF)
        e = jnp.exp(sm - jnp.max(sm, axis=0, keepdims=True))
        pb = jnp.where(cmask, e / _col_sum(e), 0.0).astype(BF16)
        o_c = _nn(vc_ref[...], pb)[grp]
        imp_h = _nn(ov_ref[...], pb)
        imp = imp_h[:, 0:QB]
        for hh in range(1, B_HPG):
            imp = imp + imp_h[:, hh * QB:(hh + 1) * QB]

        forced = (n_idx == lax.shift_right_logical(t_q, SEL_SHIFT)) | (n_idx == 0)
        admissible = n_idx * SEL_BLOCK <= t_q
        key = _sort_key(jnp.where(forced, FORCED_SCORE, jnp.where(admissible, imp, -1.0)))
        tau, n_gt_tie = _kth_largest_small(key, SEL_TOPN)
        eq = key == tau
        need = jnp.where(n_gt_tie >= 0, (SEL_TOPN - n_gt_tie).astype(F32), NO_LIMIT)
        rank = _nn(tril_ref[...], jnp.where(eq, 1.0, 0.0).astype(BF16))
        sel_scr[...] = jnp.where((key > tau) | (eq & (rank <= need)), 1.0, 0.0)

        m_scr[...] = jnp.full(m_scr.shape, NEG_INF, F32)
        acc_scr[...] = jnp.zeros_like(acc_scr)

        def sel_chunk(c):
            base = pl.multiple_of(c * KCH, KCH)
            blocks = [jnp.broadcast_to(sel_scr[pl.ds(c * (KCH // SEL_BLOCK) + j, 1), :], (SEL_BLOCK, QB))
                      for j in range(KCH // SEL_BLOCK)]
            msk = (jnp.concatenate(blocks, axis=0) > 0.5) & (base + k_pos <= q_pos)
            return sk_ref[pl.ds(base, KCH), :], msk, sv_ref[g, c]

        def sel_step(cs, carry):
            _online_softmax_step([sel_chunk(c) for c in cs], q, B_HPG, m_scr, acc_scr)
            return carry

        _for_chunk_pairs(nch, sel_step, 0)
        acc = acc_scr[...]
        o_s = acc[0:HEAD_DIM] / acc[HEAD_DIM:HEAD_DIM + 1]

        wlen = QB + WINDOW
        blk0 = jnp.maximum(i - WINDOW // QB, 0)
        start = pl.multiple_of(blk0 * QB, QB)
        sw = _nt(wk_ref[pl.ds(start, wlen), :], q)
        w_idx = start + lax.broadcasted_iota(I32, (wlen, cols), 0)
        t_w = t_of_col((wlen, cols))
        wmask = (w_idx <= t_w) & (w_idx > t_w - WINDOW)
        swm = jnp.where(wmask, sw, NEG_INF)
        pw = jnp.exp(swm - jnp.max(swm, axis=0, keepdims=True)).astype(BF16)
        accw = jnp.zeros((VROWS, cols), F32)
        for j in range(wlen // LANES):
            accw = accw + _nn(wv_ref[g, blk0 + j], pw[j * LANES:(j + 1) * LANES])
        o_w = accw[0:HEAD_DIM] / accw[HEAD_DIM:HEAD_DIM + 1]

        for hh in range(B_HPG):
            h = g * B_HPG + hh
            cs = slice(hh * QB, (hh + 1) * QB)
            r = h * N_BRANCH_GATES
            heads[h] = (gates[r:r + 1] * o_c[:, cs] + gates[r + 1:r + 2] * o_s[:, cs]
                        + gates[r + 2:r + 3] * o_w[:, cs])

    _store_heads(o_ref, heads)


def _nsa(bq, kc, vc, sk, sv, wk, wv, gt, ov, tril, batch, seq):
    nqb = seq // QB
    nkc = seq // KCH
    ncmp = seq // CMP_STRIDE
    sspec = pl.BlockSpec((1, 8, QB, LANES), lambda b, i: (b * nqb + i, 0, 0, 0))
    kspec = pl.BlockSpec((seq, LANES), lambda b, i: (b, 0))
    return pl.pallas_call(
        _nsa_body,
        grid=(batch, nqb),
        in_specs=[sspec,
                  pl.BlockSpec((ncmp, LANES), lambda b, i: (b, 0)),
                  pl.BlockSpec((LANES, ncmp), lambda b, i: (b, 0)),
                  kspec,
                  pl.BlockSpec((B_GROUPS, nkc, VROWS, KCH), lambda b, i: (0, b, 0, 0)),
                  kspec,
                  pl.BlockSpec((B_GROUPS, seq // LANES, VROWS, LANES), lambda b, i: (0, b, 0, 0)),
                  pl.BlockSpec((GATE_ROWS, QB), lambda b, i: (0, b * nqb + i)),
                  pl.BlockSpec((NSEL, ncmp), lambda b, i: (0, 0)),
                  pl.BlockSpec((NSEL, NSEL), lambda b, i: (0, 0))],
        out_specs=pl.BlockSpec((QB, B_HEADS * HEAD_DIM), lambda b, i: (b * nqb + i, 0)),
        out_shape=jax.ShapeDtypeStruct((batch * seq, B_HEADS * HEAD_DIM), BF16),
        scratch_shapes=[pltpu.VMEM((1, B_HPG * QB), F32),
                        pltpu.VMEM((VROWS, B_HPG * QB), F32),
                        pltpu.VMEM((NSEL, QB), F32)],
        compiler_params=_cparams(("parallel", "arbitrary")),
        name="nsa",
    )(bq, kc, vc, sk, sv, wk, wv, gt, ov, tril)


def _merge_body(x_ref, oa_ref, ob_ref, gm_ref, wpa_ref, wpb_ref, wo_ref, o_ref):
    gm = gm_ref[...]
    ga = _sigmoid(gm[:, :D_MODEL])
    gb = _sigmoid(gm[:, D_MODEL:])
    merged = ga * _nn(oa_ref[...], wpa_ref[...]) + gb * _nn(ob_ref[...], wpb_ref[...])
    o_ref[...] = x_ref[...] + _nn(merged.astype(BF16), wo_ref[...])


def _merge(x, oa, ob, z, wpa, wpb, wo, layer, tm=512):
    m, d = x.shape
    full = lambda a: pl.BlockSpec((None,) + a.shape[1:], lambda i: (layer, 0, 0))
    return pl.pallas_call(
        _merge_body,
        grid=(m // tm,),
        in_specs=[pl.BlockSpec((tm, d), lambda i: (i, 0)),
                  pl.BlockSpec((tm, oa.shape[1]), lambda i: (i, 0)),
                  pl.BlockSpec((tm, ob.shape[1]), lambda i: (i, 0)),
                  pl.BlockSpec((tm, 2 * d), lambda i: (i, OFF_GM // (2 * d))),
                  full(wpa), full(wpb), full(wo)],
        out_specs=pl.BlockSpec((tm, d), lambda i: (i, 0)),
        out_shape=jax.ShapeDtypeStruct((m, d), F32),
        compiler_params=_cparams(("parallel",)),
        name="merge",
    )(x, oa, ob, z, wpa, wpb, wo)


def _ple_body(x_ref, p_ref, g_ref, wg_ref, wp_ref, fn_ref, o_ref, *, final):
    x = x_ref[...]
    gate = _sigmoid(_nn(_rms(x, g_ref[...]).astype(BF16), wg_ref[...]))
    y = x + gate * _nn(p_ref[...].astype(BF16), wp_ref[...])
    o_ref[...] = _rms(y, fn_ref[...]) if final else y


def _ple(x, p, g, wg, wp, fn, layer, final, tm=512):
    m, d = x.shape
    full = lambda a: pl.BlockSpec((None,) + a.shape[1:], lambda i: (layer, 0, 0))
    return pl.pallas_call(
        functools.partial(_ple_body, final=final),
        grid=(m // tm,),
        in_specs=[pl.BlockSpec((tm, d), lambda i: (i, 0)),
                  pl.BlockSpec((None, tm, p.shape[2]), lambda i: (layer, i, 0)),
                  full(g), full(wg), full(wp), pl.BlockSpec(fn.shape, lambda i: (0, 0))],
        out_specs=pl.BlockSpec((tm, d), lambda i: (i, 0)),
        out_shape=jax.ShapeDtypeStruct((m, d), F32),
        compiler_params=_cparams(("parallel",)),
        name="ple",
    )(x, p, g, wg, wp, fn)


def _pad_w_in(w_in):
    sizes = (512, 128, 256, 32, 8, 512, 768, 24, 2048)
    offs = (OFF_QA, OFF_CKV, OFF_QI, OFF_KI, OFF_WI, OFF_QB, OFF_KVB, OFF_GB, OFF_GM)
    out = jnp.zeros(w_in.shape[:2] + (Z_COLS,), BF16)
    src = 0
    for size, off in zip(sizes, offs):
        out = out.at[:, :, off:off + size].set(w_in[:, :, src:src + size].astype(BF16))
        src += size
    return out


def _cmp_w1(w1):
    nl = w1.shape[0]
    w = w1.reshape(nl, 2, CMP_STRIDE, HEAD_DIM, CMP_HIDDEN).astype(BF16)
    out = jnp.zeros((nl, B_GROUPS, 2, CMP_STRIDE, B_GROUPS, HEAD_DIM, CMP_HIDDEN), BF16)
    for g in range(B_GROUPS):
        out = out.at[:, g, :, :, g].set(w)
    return out.reshape(nl, B_GROUPS, 2, CMP_STRIDE * LANES, CMP_HIDDEN)


def _cmp_w2(w2):
    nl = w2.shape[0]
    out = jnp.zeros((nl, B_GROUPS, CMP_HIDDEN, B_GROUPS, HEAD_DIM), BF16)
    for g in range(B_GROUPS):
        out = out.at[:, g, :, g].set(w2.astype(BF16))
    return out.reshape(nl, B_GROUPS, CMP_HIDDEN, LANES)


def _cmp_pos(cmp_pos):
    nl = cmp_pos.shape[0]
    p = cmp_pos.reshape(nl, 2, CMP_STRIDE, 1, HEAD_DIM)
    return jnp.broadcast_to(p, (nl, 2, CMP_STRIDE, B_GROUPS, HEAD_DIM)).reshape(nl, 2, CMP_STRIDE * LANES)


def _masks(seq):
    ncmp = seq // CMP_STRIDE
    c = np.arange(ncmp)[:, None] * CMP_STRIDE
    n = np.arange(NSEL)[None, :] * SEL_BLOCK
    n_real = (seq - CMP_BLOCK) // CMP_STRIDE + 1
    overlap = (c < n + SEL_BLOCK) & (n < c + CMP_BLOCK) & (np.arange(ncmp)[:, None] < n_real)
    tril = lambda n: np.arange(n)[None, :] <= np.arange(n)[:, None]
    return (jnp.asarray(overlap.T, BF16), jnp.asarray(tril(KCH), BF16), jnp.asarray(tril(NSEL), BF16))


def kernel(x, p, ffa_norm, ffa_w1, ffa_w3, ffa_w2, mix_norm, w_in, a_kv_norm, a_w_ukv, cmp_pos, cmp_k_w1, cmp_k_w2, cmp_v_w1, cmp_v_w2, w_proj_a, w_proj_b, w_out, ffb_norm, ffb_w1, ffb_w3, ffb_w2, ple_norm, ple_w_gate, ple_w_proj, final_norm):
    batch, seq, d = x.shape
    depth = w_in.shape[0]
    m = batch * seq
    bf = lambda a: a.astype(BF16)
    ffa_w1, ffa_w3, ffa_w2 = bf(ffa_w1), bf(ffa_w3), bf(ffa_w2)
    ffb_w1, ffb_w3, ffb_w2 = bf(ffb_w1), bf(ffb_w3), bf(ffb_w2)
    w_in_p = _pad_w_in(w_in)
    ukv = bf(a_w_ukv)
    wk1, wk2 = _cmp_w1(cmp_k_w1), _cmp_w2(cmp_k_w2)
    wv1, wv2 = _cmp_w1(cmp_v_w1), _cmp_w2(cmp_v_w2)
    pos2 = _cmp_pos(cmp_pos)
    wpa, wpb, wo = bf(w_proj_a), bf(w_proj_b), bf(w_out)
    wg, wp = bf(ple_w_gate), bf(ple_w_proj)
    tabs = _rope_tables(seq)
    overlap, tril_k, tril_n = _masks(seq)
    fn = final_norm.reshape(1, d)

    xf = x.reshape(m, d)
    pf = p.reshape(depth, m, p.shape[-1])
    row3 = lambda a: a.reshape(depth, 1, d)
    ffa_norm, ffb_norm, mix_norm, ple_norm = row3(ffa_norm), row3(ffb_norm), row3(mix_norm), row3(ple_norm)
    for i in range(depth):
        xf = _ffn(xf, ffa_norm, ffa_w1, ffa_w3, ffa_w2, i)
        z = _inproj(xf, mix_norm, w_in_p, i)
        (aq, ak, av, iq, ik, iw, bq, ck, cv, sk, sv, wk, wv, gt) = _prep(
            z, tabs, a_kv_norm[i].reshape(1, A_KV_RANK), ukv[i], seq)
        kc, vc = _compress(ck, cv, pos2[i], wk1[i], wk2[i], wv1[i], wv2[i], batch, seq)
        oa = _dsa(aq, iq, iw, ak, av, ik, tril_k, batch, seq)
        ob = _nsa(bq, kc, vc, sk, sv, wk, wv, gt, overlap, tril_n, batch, seq)
        xf = _merge(xf, oa, ob, z, wpa, wpb, wo, i)
        xf = _ffn(xf, ffb_norm, ffb_w1, ffb_w3, ffb_w2, i)
        xf = _ple(xf, pf, ple_norm, wg, wp, fn, i, final=(i == depth - 1))
    return xf.reshape(batch, seq, d)
```
